```python
import jax, jax.numpy as jnp
from jax import lax
import numpy as np


D_MODEL = 1024
BATCH = 2
SEQ = 16384
DEPTH = 2

N_MIXERS = 2
FOX_HEADS = 16
FOX_HEAD_DIM = D_MODEL // FOX_HEADS
Q_BLOCK = 128
SSM_D_INNER = 2 * D_MODEL
SSM_HEAD_DIM = 64
SSM_HEADS = SSM_D_INNER // SSM_HEAD_DIM
SSM_GROUPS = 8
SSM_HEADS_PER_GROUP = SSM_HEADS // SSM_GROUPS
SSM_STATE = 128
SSM_CONV = 4
SSM_CHUNK = 128
SSM_CONV_DIM = SSM_D_INNER + 2 * SSM_GROUPS * SSM_STATE
SSM_IN_DIM = SSM_D_INNER + SSM_CONV_DIM + SSM_HEADS
D_FF = 4 * D_MODEL
LN_EPS = 1e-5
RMS_EPS = 1e-5

kernel_name = 'hybrid_fox_ssd_sqrelu_deepnorm_adaln'


def layer_norm(x, g, b):
    xf = x.astype(jnp.float32)
    mu = jnp.mean(xf, axis=-1, keepdims=True)
    var = jnp.mean(jnp.square(xf - mu), axis=-1, keepdims=True)
    return ((xf - mu) * lax.rsqrt(var + LN_EPS) * g.astype(jnp.float32) + b.astype(jnp.float32)).astype(x.dtype)


def fox_attention(u, w_in, b_f, w_o):
    bsz, s, _ = u.shape
    h, dh = FOX_HEADS, FOX_HEAD_DIM
    proj = u @ w_in
    q, k, v, f = jnp.split(proj, [D_MODEL, 2 * D_MODEL, 3 * D_MODEL], axis=-1)
    q = q.reshape(bsz, s, h, dh).transpose(0, 2, 1, 3)
    k = k.reshape(bsz, s, h, dh).transpose(0, 2, 1, 3)
    v = v.reshape(bsz, s, h, dh).transpose(0, 2, 1, 3)
    log_f = jax.nn.log_sigmoid((f + b_f).astype(jnp.float32))
    cum = jnp.cumsum(log_f, axis=1).transpose(0, 2, 1)
    nb = s // Q_BLOCK
    q_blocks = q.reshape(bsz, h, nb, Q_BLOCK, dh).transpose(2, 0, 1, 3, 4)
    cum_blocks = cum.reshape(bsz, h, nb, Q_BLOCK).transpose(2, 0, 1, 3)
    pos_blocks = jnp.arange(s, dtype=jnp.int32).reshape(nb, Q_BLOCK)
    key_pos = jnp.arange(s, dtype=jnp.int32)
    scale = dh ** -0.5

    def block(args):
        qb, cb, pb = args
        logits = jnp.einsum('bhqd,bhkd->bhqk', qb, k).astype(jnp.float32) * scale
        logits = logits + cb[..., :, None] - cum[:, :, None, :]
        causal = pb[:, None] >= key_pos[None, :]
        logits = jnp.where(causal, logits, -jnp.inf)
        p = jax.nn.softmax(logits, axis=-1).astype(v.dtype)
        return jnp.einsum('bhqk,bhkd->bhqd', p, v)

    out = lax.map(block, (q_blocks, cum_blocks, pos_blocks))
    out = out.transpose(1, 0, 3, 2, 4).reshape(bsz, s, D_MODEL)
    return out @ w_o


def causal_depthwise_conv(x, w, b):
    kw, ch = w.shape
    y = lax.conv_general_dilated(x, w[:, None, :], window_strides=(1,), padding=[(kw - 1, 0)],
                                 dimension_numbers=('NWC', 'WIO', 'NWC'), feature_group_count=ch)
    return y + b


def ssd_mixer(u, w_in, conv_w, conv_b, dt_bias, a_log, d_skip, norm_w, w_out):
    bsz, s, _ = u.shape
    g, r, n, p, l = SSM_GROUPS, SSM_HEADS_PER_GROUP, SSM_STATE, SSM_HEAD_DIM, SSM_CHUNK
    f32 = jnp.float32
    proj = u @ w_in
    z, xbc, dt = jnp.split(proj, [SSM_D_INNER, SSM_D_INNER + SSM_CONV_DIM], axis=-1)
    xbc = jax.nn.silu(causal_depthwise_conv(xbc, conv_w, conv_b))
    xs, bm, cm = jnp.split(xbc, [SSM_D_INNER, SSM_D_INNER + g * n], axis=-1)
    nc = s // l
    xc = xs.astype(f32).reshape(bsz, nc, l, g, r, p)
    bc = bm.astype(f32).reshape(bsz, nc, l, g, n)
    cc = cm.astype(f32).reshape(bsz, nc, l, g, n)
    dt = jax.nn.softplus((dt + dt_bias).astype(f32))
    a = -jnp.exp(a_log.astype(f32))
    dt_c = dt.reshape(bsz, nc, l, g, r)
    a_cs = jnp.cumsum((dt_c * a.reshape(g, r)).transpose(0, 1, 3, 4, 2), axis=-1)
    xdt = xc * dt_c[..., None]
    cb = jnp.einsum('bclgn,bcsgn->bcgls', cc, bc)
    seg = a_cs[..., :, None] - a_cs[..., None, :]
    tril = jnp.tril(jnp.ones((l, l), dtype=bool))
    decay_in = jnp.exp(jnp.where(tril, seg, -jnp.inf))
    y_diag = jnp.einsum('bcgls,bcgrls,bcsgrp->bclgrp', cb, decay_in, xdt)
    decay_to_end = jnp.exp(a_cs[..., -1:] - a_cs)
    states = jnp.einsum('bclgn,bcgrl,bclgrp->bcgrpn', bc, decay_to_end, xdt)
    chunk_decay = jnp.exp(a_cs[..., -1])

    def step(hst, inp):
        s_c, d_c = inp
        return hst * d_c[..., None, None] + s_c, hst

    h0 = jnp.zeros_like(states[:, 0])
    _, prev = lax.scan(step, h0, (states.transpose(1, 0, 2, 3, 4, 5), chunk_decay.transpose(1, 0, 2, 3)))
    prev = prev.transpose(1, 0, 2, 3, 4, 5)
    y_off = jnp.einsum('bclgn,bcgrpn,bcgrl->bclgrp', cc, prev, jnp.exp(a_cs))
    y = y_diag + y_off + xc * d_skip.astype(f32).reshape(g, r)[:, :, None]
    y = y.reshape(bsz, s, SSM_D_INNER)
    yg = (y * jax.nn.silu(z.astype(f32))).reshape(bsz, s, g, SSM_D_INNER // g)
    yg = yg * lax.rsqrt(jnp.mean(jnp.square(yg), axis=-1, keepdims=True) + RMS_EPS)
    y = (yg.reshape(bsz, s, SSM_D_INNER) * norm_w.astype(f32)).astype(u.dtype)
    return y @ w_out


def sq_relu_mlp(u, w1, w2):
    return jnp.square(jax.nn.relu(u @ w1)) @ w2


def setup_inputs(seed: int = 0) -> dict:
    key = jax.random.key(seed)
    ks = jax.random.split(key, 24)
    n_a = (DEPTH + 1) // 2
    n_b = DEPTH // 2
    beta = (8.0 * DEPTH) ** -0.25
    f32 = jnp.float32

    def nrm(k, shape, fan_in, s=1.0):
        return jax.random.normal(k, shape, f32) * (s * fan_in ** -0.5)

    def small(k, shape, s=0.02):
        return jax.random.normal(k, shape, f32) * s

    dt0 = jnp.exp(jax.random.uniform(ks[17], (n_b, SSM_HEADS), f32, np.log(1e-3), np.log(1e-1)))
    dt_bias = dt0 + jnp.log(-jnp.expm1(-dt0))
    a_log = jnp.log(jax.random.uniform(ks[18], (n_b, SSM_HEADS), f32, 1.0, 16.0))
    return {
        'x': jax.random.normal(ks[0], (BATCH, SEQ, D_MODEL), f32),
        'c': jax.random.normal(ks[1], (BATCH, D_MODEL), f32),
        'ada_w': nrm(ks[2], (DEPTH, D_MODEL, 6 * D_MODEL), D_MODEL, 0.1),
        'ada_b': small(ks[3], (DEPTH, 6 * D_MODEL), 0.01),
        'ln_mix_g': 1.0 + small(ks[4], (DEPTH, D_MODEL)),
        'ln_mix_b': small(ks[5], (DEPTH, D_MODEL)),
        'ln_mlp_g': 1.0 + small(ks[6], (DEPTH, D_MODEL)),
        'ln_mlp_b': small(ks[7], (DEPTH, D_MODEL)),
        'mlp_w1': nrm(ks[8], (DEPTH, D_MODEL, D_FF), D_MODEL),
        'mlp_w2': nrm(ks[9], (DEPTH, D_FF, D_MODEL), D_FF, beta),
        'fox_w_in': nrm(ks[10], (n_a, D_MODEL, 3 * D_MODEL + FOX_HEADS), D_MODEL),
        'fox_b_f': 2.0 + small(ks[11], (n_a, FOX_HEADS), 0.1),
        'fox_w_o': nrm(ks[12], (n_a, D_MODEL, D_MODEL), D_MODEL, beta),
        'ssm_w_in': nrm(ks[13], (n_b, D_MODEL, SSM_IN_DIM), D_MODEL),
        'ssm_conv_w': nrm(ks[14], (n_b, SSM_CONV, SSM_CONV_DIM), SSM_CONV),
        'ssm_conv_b': small(ks[15], (n_b, SSM_CONV_DIM)),
        'ssm_dt_bias': dt_bias,
        'ssm_a_log': a_log,
        'ssm_d': 1.0 + small(ks[16], (n_b, SSM_HEADS)),
        'ssm_norm_w': 1.0 + small(ks[19], (n_b, SSM_D_INNER)),
        'ssm_w_out': nrm(ks[20], (n_b, SSM_D_INNER, D_MODEL), SSM_D_INNER, beta),
    }


def reference(x, c, ada_w, ada_b, ln_mix_g, ln_mix_b, ln_mlp_g, ln_mlp_b, mlp_w1, mlp_w2,
              fox_w_in, fox_b_f, fox_w_o, ssm_w_in, ssm_conv_w, ssm_conv_b, ssm_dt_bias,
              ssm_a_log, ssm_d, ssm_norm_w, ssm_w_out):
    alpha = (2.0 * DEPTH) ** 0.25
    cond = jax.nn.silu(c)
    for i in range(DEPTH):
        mod = (cond @ ada_w[i] + ada_b[i])[:, None, :]
        sh_a, sc_a, g_a, sh_m, sc_m, g_m = jnp.split(mod, 6, axis=-1)
        u = x * (1.0 + sc_a) + sh_a
        j = i // N_MIXERS
        if i % N_MIXERS == 0:
            y = fox_attention(u, fox_w_in[j], fox_b_f[j], fox_w_o[j])
        else:
            y = ssd_mixer(u, ssm_w_in[j], ssm_conv_w[j], ssm_conv_b[j], ssm_dt_bias[j],
                          ssm_a_log[j], ssm_d[j], ssm_norm_w[j], ssm_w_out[j])
        x = layer_norm(alpha * x + (1.0 + g_a) * y, ln_mix_g[i], ln_mix_b[i])
        u = x * (1.0 + sc_m) + sh_m
        y = sq_relu_mlp(u, mlp_w1[i], mlp_w2[i])
        x = layer_norm(alpha * x + (1.0 + g_m) * y, ln_mlp_g[i], ln_mlp_b[i])
    return x
```

```python
import functools

import jax
import jax.numpy as jnp
from jax import lax
from jax.experimental import pallas as pl
from jax.experimental.pallas import tpu as pltpu

F32 = jnp.float32
BF16 = jnp.bfloat16

D_MODEL = 1024
DEPTH = 2
FOX_HEADS = 16
FOX_HEAD_DIM = D_MODEL // FOX_HEADS
SSM_D_INNER = 2 * D_MODEL
SSM_HEAD_DIM = 64
SSM_HEADS = SSM_D_INNER // SSM_HEAD_DIM
SSM_GROUPS = 8
SSM_HEADS_PER_GROUP = SSM_HEADS // SSM_GROUPS
SSM_STATE = 128
SSM_CONV = 4
SSM_CHUNK = 128
SSM_GROUP_WIDTH = SSM_D_INNER // SSM_GROUPS
SSM_BC_WIDTH = SSM_GROUPS * SSM_STATE
SSM_CONV_DIM = SSM_D_INNER + 2 * SSM_BC_WIDTH
D_FF = 4 * D_MODEL
LN_EPS = 1e-5
RMS_EPS = 1e-5
RESIDUAL_ALPHA = (2.0 * DEPTH) ** 0.25

LANES = 128
SUBLANES = 8
VMEM_LIMIT_BYTES = 56 * 1024 * 1024

ATTN_BLOCK = 256
ROW_TILE = 512
AUG_PER_HEAD = 6
SPLIT = 3


def _split3(v):
    hi = v.astype(BF16)
    r1 = v - hi.astype(F32)
    mid = r1.astype(BF16)
    lo = (r1 - mid.astype(F32)).astype(BF16)
    return hi, mid, lo


def _softplus(y):
    return jnp.maximum(y, 0.0) + jnp.log1p(jnp.exp(-jnp.abs(y)))


def _sigmoid(y):
    return 1.0 / (1.0 + jnp.exp(-y))


def _layer_norm(r, g, b):
    mu = jnp.mean(r, axis=-1, keepdims=True)
    cen = r - mu
    var = jnp.mean(cen * cen, axis=-1, keepdims=True)
    return cen * lax.rsqrt(var + LN_EPS) * g + b


def _compiler_params(semantics):
    return pltpu.CompilerParams(dimension_semantics=semantics, vmem_limit_bytes=VMEM_LIMIT_BYTES)


def _ada_kernel(c_ref, w_ref, b_ref, o_ref):
    c = c_ref[...]
    cond = c * _sigmoid(c)
    o_ref[0] = jnp.dot(cond, w_ref[0], precision=lax.Precision.HIGHEST,
                       preferred_element_type=F32) + b_ref[0]


def _ada_modulation(c, ada_w, ada_b):
    depth, d, n = ada_w.shape
    bsz = c.shape[0]
    tn = 1536
    out = pl.pallas_call(
        _ada_kernel,
        grid=(depth, n // tn),
        in_specs=[
            pl.BlockSpec((bsz, d), lambda i, j: (0, 0)),
            pl.BlockSpec((1, d, tn), lambda i, j: (i, 0, j)),
            pl.BlockSpec((1, 1, tn), lambda i, j: (i, 0, j)),
        ],
        out_specs=pl.BlockSpec((1, bsz, tn), lambda i, j: (i, 0, j)),
        out_shape=jax.ShapeDtypeStruct((depth, bsz, n), F32),
        compiler_params=_compiler_params(("arbitrary", "arbitrary")),
        name="ada_modulation",
    )(c, ada_w, ada_b.reshape(depth, 1, n))
    return out.reshape(depth, bsz, 6, d)


def _fox_inproj_kernel(x_ref, mod_ref, w_ref, bf_ref, tri_ref, eq_ref, ek_ref, oq_ref, ok_ref,
                       q_ref, k_ref, v_ref, qaug_ref, kaug_ref, rblk_ref, carry_ref, *, tm, gb):
    @pl.when(pl.program_id(1) == 0)
    def _():
        carry_ref[...] = jnp.zeros_like(carry_ref)

    x = x_ref[0]
    u = (x * (1.0 + mod_ref[0, 1:2, :]) + mod_ref[0, 0:1, :]).astype(BF16)
    d = D_MODEL
    q_ref[0] = (jnp.dot(u, w_ref[:, 0:d], preferred_element_type=F32)
                * (FOX_HEAD_DIM ** -0.5)).astype(BF16)
    k_ref[0] = jnp.dot(u, w_ref[:, d:2 * d], preferred_element_type=F32).astype(BF16)
    v_ref[0] = jnp.dot(u, w_ref[:, 2 * d:3 * d], preferred_element_type=F32).astype(BF16)
    f = jnp.dot(u, w_ref[:, 3 * d:3 * d + LANES], preferred_element_type=F32) + bf_ref[...]
    logf = jnp.minimum(f, 0.0) - jnp.log1p(jnp.exp(-jnp.abs(f)))

    for g in range(tm // gb):
        lf = logf[g * gb:(g + 1) * gb]
        rel = jnp.dot(tri_ref[...], jnp.concatenate(_split3(lf), axis=0), preferred_element_type=F32)
        rcat = jnp.concatenate(_split3(rel), axis=1)
        qaug_ref[0, g * gb:(g + 1) * gb, :] = (
            jnp.dot(rcat, eq_ref[...], preferred_element_type=F32) + oq_ref[...]).astype(BF16)
        kaug_ref[0, g * gb:(g + 1) * gb, :] = (
            jnp.dot(rcat, ek_ref[...], preferred_element_type=F32) + ok_ref[...]).astype(BF16)
        rblk_ref[0, 0, g:g + 1, :] = carry_ref[...]
        carry_ref[...] = carry_ref[...] + rel[gb - 1:gb, :]


def _fox_aug_constants(gb):
    lane = jnp.arange(LANES)
    row = jnp.arange(SPLIT * LANES)
    piece, head = row // LANES, row % LANES
    valid = head < FOX_HEADS
    eq = (valid[:, None] & (lane[None, :] == (AUG_PER_HEAD * head + piece)[:, None])).astype(BF16)
    ek = -(valid[:, None] & (lane[None, :] == (AUG_PER_HEAD * head + SPLIT + piece)[:, None])).astype(BF16)
    used = lane < AUG_PER_HEAD * FOX_HEADS
    oq = (used & (lane % AUG_PER_HEAD >= SPLIT)).astype(F32)[None, :]
    ok = (used & (lane % AUG_PER_HEAD < SPLIT)).astype(F32)[None, :]
    t = jnp.arange(gb)
    tri = (t[:, None] >= t[None, :]).astype(BF16)
    tri3 = jnp.concatenate([tri] * SPLIT, axis=1)
    return tri3, eq, ek, oq, ok


def _fox_inproj(x, mod, w_in, b_f):
    bsz, s, d = x.shape
    tm, gb = min(ROW_TILE, s), ATTN_BLOCK
    n_pad = 3 * d + LANES
    w = jnp.pad(w_in, ((0, 0), (0, n_pad - w_in.shape[1]))).astype(BF16)
    bf = jnp.pad(b_f, (0, LANES - FOX_HEADS)).reshape(1, LANES).astype(F32)
    tri3, eq, ek, oq, ok = _fox_aug_constants(gb)
    const = lambda shape: pl.BlockSpec(shape, lambda b, i: (0,) * len(shape))
    rows = lambda width: pl.BlockSpec((1, tm, width), lambda b, i: (b, i, 0))
    return pl.pallas_call(
        functools.partial(_fox_inproj_kernel, tm=tm, gb=gb),
        grid=(bsz, s // tm),
        in_specs=[rows(d), pl.BlockSpec((1, 6, d), lambda b, i: (b, 0, 0)), const(w.shape),
                  const(bf.shape), const(tri3.shape), const(eq.shape), const(ek.shape),
                  const(oq.shape), const(ok.shape)],
        out_specs=[rows(d), rows(d), rows(d), rows(LANES), rows(LANES),
                   pl.BlockSpec((1, 1, tm // gb, LANES), lambda b, i: (b, i, 0, 0))],
        out_shape=[jax.ShapeDtypeStruct((bsz, s, d), BF16)] * 3
        + [jax.ShapeDtypeStruct((bsz, s, LANES), BF16)] * 2
        + [jax.ShapeDtypeStruct((bsz, s // tm, tm // gb, LANES), F32)],
        scratch_shapes=[pltpu.VMEM((1, LANES), F32)],
        compiler_params=_compiler_params(("arbitrary", "arbitrary")),
        name="fox_inproj",
    )(x, mod, w, bf, tri3, eq, ek, oq, ok)


def _fox_attn_kernel(r_ref, q_ref, qaug_ref, k_ref, v_ref, kaug_ref, o_ref, *, blk, nblk):
    b, hp, i = pl.program_id(0), pl.program_id(1), pl.program_id(2)
    lane = lax.broadcasted_iota(jnp.int32, (1, LANES), 1)
    q = q_ref[0]
    qa = qaug_ref[0]
    zero = jnp.zeros((), BF16)
    nt = (((1,), (1,)), ((), ()))

    qcat, vmask, rbase = [], [], []
    for hh in range(2):
        h = 2 * hp + hh
        own = (lane >= hh * FOX_HEAD_DIM) & (lane < (hh + 1) * FOX_HEAD_DIM)
        aug = (lane >= AUG_PER_HEAD * h) & (lane < AUG_PER_HEAD * (h + 1))
        qcat.append(jnp.concatenate([jnp.where(own, q, zero), jnp.where(aug, qa, zero)], axis=1))
        vmask.append(own)
        rbase.append((b * FOX_HEADS + h) * nblk)

    def tile(j, carry, diagonal):
        start = pl.multiple_of(j * blk, blk)
        ks = k_ref[0, pl.ds(start, blk), :]
        ka = kaug_ref[0, pl.ds(start, blk), :]
        vs = v_ref[0, pl.ds(start, blk), :]
        kcat = jnp.concatenate([ks, ka], axis=1)
        out = []
        for hh in range(2):
            m, l, acc = carry[hh]
            z = lax.dot_general(qcat[hh], kcat, nt, preferred_element_type=F32)
            dij = r_ref[rbase[hh] + i] - r_ref[rbase[hh] + j]
            if diagonal:
                rr = lax.broadcasted_iota(jnp.int32, (blk, blk), 0)
                cc = lax.broadcasted_iota(jnp.int32, (blk, blk), 1)
                z = jnp.where(rr >= cc, z, -jnp.inf)
            m_new = jnp.maximum(m, jnp.max(z, axis=1, keepdims=True) + dij)
            p = jnp.exp(z - (m_new - dij))
            alpha = jnp.exp(m - m_new)
            l = alpha * l + jnp.sum(p, axis=1, keepdims=True)
            vm = jnp.where(vmask[hh], vs, zero)
            acc = alpha * acc + jnp.dot(p.astype(BF16), vm, preferred_element_type=F32)
            out.append((m_new, l, acc))
        return tuple(out)

    init = tuple((jnp.full((blk, 1), -jnp.inf, F32), jnp.zeros((blk, 1), F32),
                  jnp.zeros((blk, LANES), F32)) for _ in range(2))
    carry = lax.fori_loop(0, i, lambda j, c: tile(j, c, False), init)
    carry = tile(i, carry, True)
    (_, l0, a0), (_, l1, a1) = carry
    o_ref[0] = (a0 / l0 + a1 / l1).astype(o_ref.dtype)


def _fox_attention(q, k, v, qaug, kaug, rblk):
    bsz, s, d = q.shape
    blk = ATTN_BLOCK
    nblk = s // blk
    rflat = rblk.reshape(bsz, nblk, LANES)[:, :, :FOX_HEADS].transpose(0, 2, 1).reshape(-1)
    grid_spec = pltpu.PrefetchScalarGridSpec(
        num_scalar_prefetch=1,
        grid=(bsz, FOX_HEADS // 2, nblk),
        in_specs=[
            pl.BlockSpec((1, blk, LANES), lambda b, hp, i, r: (b, i, hp)),
            pl.BlockSpec((1, blk, LANES), lambda b, hp, i, r: (b, i, 0)),
            pl.BlockSpec((1, s, LANES), lambda b, hp, i, r: (b, 0, hp)),
            pl.BlockSpec((1, s, LANES), lambda b, hp, i, r: (b, 0, hp)),
            pl.BlockSpec((1, s, LANES), lambda b, hp, i, r: (b, 0, 0)),
        ],
        out_specs=pl.BlockSpec((1, blk, LANES), lambda b, hp, i, r: (b, i, hp)),
    )
    return pl.pallas_call(
        functools.partial(_fox_attn_kernel, blk=blk, nblk=nblk),
        grid_spec=grid_spec,
        out_shape=jax.ShapeDtypeStruct((bsz, s, d), BF16),
        compiler_params=_compiler_params(("arbitrary", "arbitrary", "arbitrary")),
        name="fox_attention",
    )(rflat, q, qaug, k, v, kaug)


def _proj_ln_kernel(a_ref, w_ref, x_ref, mod_ref, g_ref, b_ref, o_ref, *, gate_row):
    y = jnp.dot(a_ref[0], w_ref[...], preferred_element_type=F32)
    r = RESIDUAL_ALPHA * x_ref[0] + (1.0 + mod_ref[0, gate_row:gate_row + 1, :]) * y
    o_ref[0] = _layer_norm(r, g_ref[...], b_ref[...])


def _proj_residual_ln(a, w, x, mod, ln_g, ln_b, gate_row):
    bsz, s, d = x.shape
    kdim = a.shape[-1]
    tm = min(ROW_TILE, s)
    return pl.pallas_call(
        functools.partial(_proj_ln_kernel, gate_row=gate_row),
        grid=(bsz, s // tm),
        in_specs=[
            pl.BlockSpec((1, tm, kdim), lambda b, i: (b, i, 0)),
            pl.BlockSpec((kdim, d), lambda b, i: (0, 0)),
            pl.BlockSpec((1, tm, d), lambda b, i: (b, i, 0)),
            pl.BlockSpec((1, 6, d), lambda b, i: (b, 0, 0)),
            pl.BlockSpec((1, d), lambda b, i: (0, 0)),
            pl.BlockSpec((1, d), lambda b, i: (0, 0)),
        ],
        out_specs=pl.BlockSpec((1, tm, d), lambda b, i: (b, i, 0)),
        out_shape=jax.ShapeDtypeStruct((bsz, s, d), F32),
        compiler_params=_compiler_params(("arbitrary", "arbitrary")),
        name="proj_residual_ln",
    )(a, w.astype(BF16), x, mod, ln_g.reshape(1, d), ln_b.reshape(1, d))


def _mlp_ln_kernel(x_ref, mod_ref, w1_ref, w2_ref, g_ref, b_ref, o_ref, *, ff_chunk):
    x = x_ref[0]
    u = (x * (1.0 + mod_ref[0, 4:5, :]) + mod_ref[0, 3:4, :]).astype(BF16)
    y = None
    for c in range(D_FF // ff_chunk):
        h = jnp.dot(u, w1_ref[:, c * ff_chunk:(c + 1) * ff_chunk], preferred_element_type=F32)
        h = jnp.square(jnp.maximum(h, 0.0)).astype(BF16)
        part = jnp.dot(h, w2_ref[c * ff_chunk:(c + 1) * ff_chunk, :], preferred_element_type=F32)
        y = part if y is None else y + part
    r = RESIDUAL_ALPHA * x + (1.0 + mod_ref[0, 5:6, :]) * y
    o_ref[0] = _layer_norm(r, g_ref[...], b_ref[...])


def _mlp_residual_ln(x, mod, w1, w2, ln_g, ln_b):
    bsz, s, d = x.shape
    tm = min(ROW_TILE, s)
    return pl.pallas_call(
        functools.partial(_mlp_ln_kernel, ff_chunk=1024),
        grid=(bsz, s // tm),
        in_specs=[
            pl.BlockSpec((1, tm, d), lambda b, i: (b, i, 0)),
            pl.BlockSpec((1, 6, d), lambda b, i: (b, 0, 0)),
            pl.BlockSpec((d, D_FF), lambda b, i: (0, 0)),
            pl.BlockSpec((D_FF, d), lambda b, i: (0, 0)),
            pl.BlockSpec((1, d), lambda b, i: (0, 0)),
            pl.BlockSpec((1, d), lambda b, i: (0, 0)),
        ],
        out_specs=pl.BlockSpec((1, tm, d), lambda b, i: (b, i, 0)),
        out_shape=jax.ShapeDtypeStruct((bsz, s, d), F32),
        compiler_params=_compiler_params(("arbitrary", "arbitrary")),
        name="mlp_residual_ln",
    )(x, mod, w1.astype(BF16), w2.astype(BF16), ln_g.reshape(1, d), ln_b.reshape(1, d))


def _ssd_inproj_kernel(x_ref, mod_ref, w_ref, z_ref, xbc_ref, dt_ref, *, chunk):
    u = (x_ref[0] * (1.0 + mod_ref[0, 1:2, :]) + mod_ref[0, 0:1, :]).astype(BF16)
    for c in range(SSM_D_INNER // chunk):
        z_ref[0, :, c * chunk:(c + 1) * chunk] = jnp.dot(
            u, w_ref[:, c * chunk:(c + 1) * chunk], preferred_element_type=F32)
    off = SSM_D_INNER
    for c in range(SSM_CONV_DIM // chunk):
        xbc_ref[0, :, c * chunk:(c + 1) * chunk] = jnp.dot(
            u, w_ref[:, off + c * chunk:off + (c + 1) * chunk], preferred_element_type=F32)
    off = SSM_D_INNER + SSM_CONV_DIM
    dt_ref[0] = jnp.dot(u, w_ref[:, off:off + LANES], preferred_element_type=F32)


def _ssd_inproj(x, mod, w_in):
    bsz, s, d = x.shape
    tm = min(ROW_TILE, s)
    n_pad = SSM_D_INNER + SSM_CONV_DIM + LANES
    w = jnp.pad(w_in, ((0, 0), (0, n_pad - w_in.shape[1]))).astype(BF16)
    rows = lambda width: pl.BlockSpec((1, tm, width), lambda b, i: (b, i, 0))
    return pl.pallas_call(
        functools.partial(_ssd_inproj_kernel, chunk=1024),
        grid=(bsz, s // tm),
        in_specs=[rows(d), pl.BlockSpec((1, 6, d), lambda b, i: (b, 0, 0)),
                  pl.BlockSpec(w.shape, lambda b, i: (0, 0))],
        out_specs=[rows(SSM_D_INNER), rows(SSM_CONV_DIM), rows(LANES)],
        out_shape=[jax.ShapeDtypeStruct((bsz, s, SSM_D_INNER), F32),
                   jax.ShapeDtypeStruct((bsz, s, SSM_CONV_DIM), F32),
                   jax.ShapeDtypeStruct((bsz, s, LANES), F32)],
        compiler_params=_compiler_params(("arbitrary", "arbitrary")),
        name="ssd_inproj",
    )(x, mod, w)


def _ssd_core_kernel(xbc_ref, z_ref, dt_ref, convw_ref, convb_ref, dtb_ref, alog_ref, dskip_ref,
                     normw_ref, tri_ref, ea_ref, eb_ref, oa_ref, ob_ref, xp_ref,
                     o_ref, ext_ref, state_ref):
    ln = SSM_CHUNK
    halo = SUBLANES

    @pl.when(pl.program_id(1) == 0)
    def _():
        ext_ref[0:halo, :] = jnp.zeros((halo, SSM_CONV_DIM), F32)
        state_ref[...] = jnp.zeros_like(state_ref)

    ext_ref[halo:halo + ln, :] = xbc_ref[0]
    conv = convb_ref[...]
    for kk in range(SSM_CONV):
        shift = halo - (SSM_CONV - 1) + kk
        conv = conv + ext_ref[shift:shift + ln, :] * convw_ref[kk:kk + 1, :]
    ext_ref[0:halo, :] = ext_ref[ln:ln + halo, :]
    xbc = conv * _sigmoid(conv)
    xs = xbc[:, :SSM_D_INNER]

    dt = _softplus(dt_ref[0] + dtb_ref[...])
    dta = dt * (-jnp.exp(alog_ref[...]))
    acs = jnp.dot(tri_ref[...], jnp.concatenate(_split3(dta), axis=0), preferred_element_type=F32)
    acs_last = acs[ln - 1:ln, :]
    eacs = jnp.exp(acs)
    dtdte = dt * jnp.exp(acs_last - acs)

    acat = jnp.concatenate(_split3(acs), axis=1)
    a_aug = (jnp.dot(acat, ea_ref[...], preferred_element_type=F32) + oa_ref[...]).astype(BF16)
    b_aug = (jnp.dot(acat, eb_ref[...], preferred_element_type=F32) + ob_ref[...]).astype(BF16)

    def expand(v):
        return jnp.dot(jnp.concatenate(_split3(v), axis=1), xp_ref[...], preferred_element_type=F32)

    xdt = (xs * expand(dt)).astype(BF16)
    xdte = (xs * expand(dtdte)).astype(BF16)
    eacs_x = expand(eacs)

    rr = lax.broadcasted_iota(jnp.int32, (ln, ln), 0)
    cc = lax.broadcasted_iota(jnp.int32, (ln, ln), 1)
    causal = rr >= cc
    lane_aug = lax.broadcasted_iota(jnp.int32, (1, 2 * LANES), 1)
    lane_grp = lax.broadcasted_iota(jnp.int32, (1, SSM_GROUP_WIDTH), 1)
    zero = jnp.zeros((), BF16)
    nt = (((1,), (1,)), ((), ()))
    tn = (((0,), (0,)), ((), ()))

    for g in range(SSM_GROUPS):
        gs = slice(g * SSM_GROUP_WIDTH, (g + 1) * SSM_GROUP_WIDTH)
        bg = xbc[:, SSM_D_INNER + g * SSM_STATE:SSM_D_INNER + (g + 1) * SSM_STATE].astype(BF16)
        cg = xbc[:, SSM_D_INNER + SSM_BC_WIDTH + g * SSM_STATE:
                 SSM_D_INNER + SSM_BC_WIDTH + (g + 1) * SSM_STATE].astype(BF16)
        cb = lax.dot_general(cg, bg, nt, preferred_element_type=F32)
        xg = xdt[:, gs]
        y = None
        for r in range(SSM_HEADS_PER_GROUP):
            h = g * SSM_HEADS_PER_GROUP + r
            sel = (lane_aug >= AUG_PER_HEAD * h) & (lane_aug < AUG_PER_HEAD * (h + 1))
            seg = lax.dot_general(jnp.where(sel, a_aug, zero), b_aug, nt, preferred_element_type=F32)
            decay = jnp.exp(jnp.where(causal, seg, -jnp.inf))
            own = (lane_grp >= r * SSM_HEAD_DIM) & (lane_grp < (r + 1) * SSM_HEAD_DIM)
            part = jnp.dot((cb * decay).astype(BF16), jnp.where(own, xg, zero),
                           preferred_element_type=F32)
            y = part if y is None else y + part
        prev = state_ref[g]
        y = y + jnp.dot(cg, prev.astype(BF16), preferred_element_type=F32) * eacs_x[:, gs]
        y = y + xs[:, gs] * dskip_ref[:, gs]
        new_states = lax.dot_general(bg, xdte[:, gs], tn, preferred_element_type=F32)
        state_ref[g] = prev * eacs_x[ln - 1:ln, gs] + new_states
        zg = z_ref[0, :, gs]
        gated = y * (zg * _sigmoid(zg))
        ms = jnp.mean(gated * gated, axis=-1, keepdims=True)
        o_ref[0, :, gs] = (gated * lax.rsqrt(ms + RMS_EPS) * normw_ref[:, gs]).astype(o_ref.dtype)


def _ssd_constants():
    ln = SSM_CHUNK
    t = jnp.arange(ln)
    tri = (t[:, None] >= t[None, :]).astype(BF16)
    tri3 = jnp.concatenate([tri] * SPLIT, axis=1)
    lane = jnp.arange(2 * LANES)
    row = jnp.arange(SPLIT * LANES)
    piece, head = row // LANES, row % LANES
    valid = head < SSM_HEADS
    ea = (valid[:, None] & (lane[None, :] == (AUG_PER_HEAD * head + piece)[:, None])).astype(BF16)
    eb = -(valid[:, None] & (lane[None, :] == (AUG_PER_HEAD * head + SPLIT + piece)[:, None])).astype(BF16)
    used = lane < AUG_PER_HEAD * SSM_HEADS
    oa = (used & (lane % AUG_PER_HEAD >= SPLIT)).astype(F32)[None, :]
    ob = (used & (lane % AUG_PER_HEAD < SPLIT)).astype(F32)[None, :]
    col = jnp.arange(SSM_D_INNER)
    xp = (valid[:, None] & (head[:, None] == (col // SSM_HEAD_DIM)[None, :])).astype(BF16)
    return tri3, ea, eb, oa, ob, xp


def _ssd_core(z, xbc, dt, conv_w, conv_b, dt_bias, a_log, d_skip, norm_w):
    bsz, s, _ = z.shape
    ln = SSM_CHUNK
    pad_heads = lambda v: jnp.pad(v.astype(F32), (0, LANES - SSM_HEADS)).reshape(1, LANES)
    consts = _ssd_constants()
    const = lambda a: pl.BlockSpec(a.shape, lambda b, i: (0,) * a.ndim)
    rows = lambda width: pl.BlockSpec((1, ln, width), lambda b, i: (b, i, 0))
    small = [conv_w.astype(F32), conv_b.reshape(1, -1).astype(F32), pad_heads(dt_bias), pad_heads(a_log),
             jnp.repeat(d_skip.astype(F32), SSM_HEAD_DIM).reshape(1, -1), norm_w.reshape(1, -1).astype(F32)]
    return pl.pallas_call(
        _ssd_core_kernel,
        grid=(bsz, s // ln),
        in_specs=[rows(SSM_CONV_DIM), rows(SSM_D_INNER), rows(LANES)]
        + [const(a) for a in small] + [const(a) for a in consts],
        out_specs=rows(SSM_D_INNER),
        out_shape=jax.ShapeDtypeStruct((bsz, s, SSM_D_INNER), BF16),
        scratch_shapes=[pltpu.VMEM((SUBLANES + ln, SSM_CONV_DIM), F32),
                        pltpu.VMEM((SSM_GROUPS, SSM_STATE, SSM_GROUP_WIDTH), F32)],
        compiler_params=_compiler_params(("arbitrary", "arbitrary")),
        name="ssd_core",
    )(xbc, z, dt, *small, *consts)


def kernel(x, c, ada_w, ada_b, ln_mix_g, ln_mix_b, ln_mlp_g, ln_mlp_b, mlp_w1, mlp_w2, fox_w_in, fox_b_f, fox_w_o, ssm_w_in, ssm_conv_w, ssm_conv_b, ssm_dt_bias, ssm_a_log, ssm_d, ssm_norm_w, ssm_w_out):
    mod = _ada_modulation(c, ada_w, ada_b)

    q, k, v, qaug, kaug, rblk = _fox_inproj(x, mod[0], fox_w_in[0], fox_b_f[0])
    attn = _fox_attention(q, k, v, qaug, kaug, rblk)
    x = _proj_residual_ln(attn, fox_w_o[0], x, mod[0], ln_mix_g[0], ln_mix_b[0], gate_row=2)
    x = _mlp_residual_ln(x, mod[0], mlp_w1[0], mlp_w2[0], ln_mlp_g[0], ln_mlp_b[0])

    z, xbc, dt = _ssd_inproj(x, mod[1], ssm_w_in[0])
    y = _ssd_core(z, xbc, dt, ssm_conv_w[0], ssm_conv_b[0], ssm_dt_bias[0], ssm_a_log[0],
                  ssm_d[0], ssm_norm_w[0])
    x = _proj_residual_ln(y, ssm_w_out[0], x, mod[1], ln_mix_g[1], ln_mix_b[1], gate_row=2)
    x = _mlp_residual_ln(x, mod[1], mlp_w1[1], mlp_w2[1], ln_mlp_g[1], ln_mlp_b[1])
    return x
```

```python
import functools

import jax
import jax.numpy as jnp
from jax import lax
from jax.experimental import pallas as pl
from jax.experimental.pallas import tpu as pltpu

F32 = jnp.float32
BF16 = jnp.bfloat16

D_MODEL = 1024
DEPTH = 2
FOX_HEADS = 16
FOX_HEAD_DIM = D_MODEL // FOX_HEADS
SSM_D_INNER = 2 * D_MODEL
SSM_HEAD_DIM = 64
SSM_HEADS = SSM_D_INNER // SSM_HEAD_DIM
SSM_GROUPS = 8
SSM_HEADS_PER_GROUP = SSM_HEADS // SSM_GROUPS
SSM_STATE = 128
SSM_CONV = 4
SSM_CHUNK = 128
SSM_GROUP_WIDTH = SSM_D_INNER // SSM_GROUPS
SSM_BC_WIDTH = SSM_GROUPS * SSM_STATE
SSM_CONV_DIM = SSM_D_INNER + 2 * SSM_BC_WIDTH
D_FF = 4 * D_MODEL
LN_EPS = 1e-5
RMS_EPS = 1e-5
RESIDUAL_ALPHA = (2.0 * DEPTH) ** 0.25

LANES = 128
SUBLANES = 8
VMEM_LIMIT_BYTES = 56 * 1024 * 1024

ATTN_BLOCK = 256
ROW_TILE = 512
AUG_PER_HEAD = 6
SPLIT = 3
NORM_SLACK = 1.01
SKIP_LOG_MARGIN = 32.0
FAST_MAX_GAP = 60.0


def _split3(v):
    hi = v.astype(BF16)
    r1 = v - hi.astype(F32)
    mid = r1.astype(BF16)
    lo = (r1 - mid.astype(F32)).astype(BF16)
    return hi, mid, lo


def _softplus(y):
    return jnp.maximum(y, 0.0) + jnp.log1p(jnp.exp(-jnp.abs(y)))


def _sigmoid(y):
    return 1.0 / (1.0 + jnp.exp(-y))


def _layer_norm(r, g, b):
    mu = jnp.mean(r, axis=-1, keepdims=True)
    cen = r - mu
    var = jnp.mean(cen * cen, axis=-1, keepdims=True)
    return cen * lax.rsqrt(var + LN_EPS) * g + b


def _compiler_params(semantics):
    return pltpu.CompilerParams(dimension_semantics=semantics, vmem_limit_bytes=VMEM_LIMIT_BYTES)


def _ada_kernel(c_ref, w_ref, b_ref, o_ref):
    c = c_ref[...]
    cond = c * _sigmoid(c)
    o_ref[0] = jnp.dot(cond, w_ref[0], precision=lax.Precision.HIGHEST,
                       preferred_element_type=F32) + b_ref[0]


def _ada_modulation(c, ada_w, ada_b):
    depth, d, n = ada_w.shape
    bsz = c.shape[0]
    tn = 1536
    out = pl.pallas_call(
        _ada_kernel,
        grid=(depth, n // tn),
        in_specs=[
            pl.BlockSpec((bsz, d), lambda i, j: (0, 0)),
            pl.BlockSpec((1, d, tn), lambda i, j: (i, 0, j)),
            pl.BlockSpec((1, 1, tn), lambda i, j: (i, 0, j)),
        ],
        out_specs=pl.BlockSpec((1, bsz, tn), lambda i, j: (i, 0, j)),
        out_shape=jax.ShapeDtypeStruct((depth, bsz, n), F32),
        compiler_params=_compiler_params(("arbitrary", "arbitrary")),
        name="ada_modulation",
    )(c, ada_w, ada_b.reshape(depth, 1, n))
    return out.reshape(depth, bsz, 6, d)


def _fox_inproj_kernel(x_ref, mod_ref, w_ref, bf_ref, tri_ref, eq_ref, ek_ref, oq_ref, ok_ref, hs_ref,
                       q_ref, k_ref, v_ref, qaug_ref, kaug_ref, rblk_ref, qn_ref, kn_ref,
                       carry_ref, kmax_ref, *, tm, gb):
    @pl.when(pl.program_id(1) == 0)
    def _():
        carry_ref[...] = jnp.zeros_like(carry_ref)
        kmax_ref[...] = jnp.zeros_like(kmax_ref)

    x = x_ref[0]
    u = (x * (1.0 + mod_ref[0, 1:2, :]) + mod_ref[0, 0:1, :]).astype(BF16)
    d = D_MODEL
    qb = (jnp.dot(u, w_ref[:, 0:d], preferred_element_type=F32) * (FOX_HEAD_DIM ** -0.5)).astype(BF16)
    kb = jnp.dot(u, w_ref[:, d:2 * d], preferred_element_type=F32).astype(BF16)
    q_ref[0] = qb
    k_ref[0] = kb

    def head_norms(t):
        t = t.astype(F32)
        return jnp.sqrt(jnp.dot((t * t).astype(BF16), hs_ref[...], preferred_element_type=F32)) * NORM_SLACK

    qnorm = head_norms(qb)
    kmax_ref[...] = jnp.maximum(kmax_ref[...], jnp.max(head_norms(kb), axis=0, keepdims=True))
    kn_ref[0, 0] = kmax_ref[...]
    v_ref[0] = jnp.dot(u, w_ref[:, 2 * d:3 * d], preferred_element_type=F32).astype(BF16)
    f = jnp.dot(u, w_ref[:, 3 * d:3 * d + LANES], preferred_element_type=F32) + bf_ref[...]
    logf = jnp.minimum(f, 0.0) - jnp.log1p(jnp.exp(-jnp.abs(f)))

    for g in range(tm // gb):
        lf = logf[g * gb:(g + 1) * gb]
        rel = jnp.dot(tri_ref[...], jnp.concatenate(_split3(lf), axis=0), preferred_element_type=F32)
        rcat = jnp.concatenate(_split3(rel), axis=1)
        qaug_ref[0, g * gb:(g + 1) * gb, :] = (
            jnp.dot(rcat, eq_ref[...], preferred_element_type=F32) + oq_ref[...]).astype(BF16)
        kaug_ref[0, g * gb:(g + 1) * gb, :] = (
            jnp.dot(rcat, ek_ref[...], preferred_element_type=F32) + ok_ref[...]).astype(BF16)
        rblk_ref[0, 0, g:g + 1, :] = carry_ref[...]
        qn_ref[0, 0, g:g + 1, :] = jnp.max(qnorm[g * gb:(g + 1) * gb], axis=0, keepdims=True)
        carry_ref[...] = carry_ref[...] + rel[gb - 1:gb, :]


def _fox_aug_constants(gb):
    lane = jnp.arange(LANES)
    row = jnp.arange(SPLIT * LANES)
    piece, head = row // LANES, row % LANES
    valid = head < FOX_HEADS
    eq = (valid[:, None] & (lane[None, :] == (AUG_PER_HEAD * head + piece)[:, None])).astype(BF16)
    ek = -(valid[:, None] & (lane[None, :] == (AUG_PER_HEAD * head + SPLIT + piece)[:, None])).astype(BF16)
    used = lane < AUG_PER_HEAD * FOX_HEADS
    oq = (used & (lane % AUG_PER_HEAD >= SPLIT)).astype(F32)[None, :]
    ok = (used & (lane % AUG_PER_HEAD < SPLIT)).astype(F32)[None, :]
    t = jnp.arange(gb)
    tri = (t[:, None] >= t[None, :]).astype(BF16)
    tri3 = jnp.concatenate([tri] * SPLIT, axis=1)
    hs = (jnp.arange(D_MODEL)[:, None] // FOX_HEAD_DIM == lane[None, :]).astype(BF16)
    return tri3, eq, ek, oq, ok, hs


def _fox_inproj(x, mod, w_in, b_f):
    bsz, s, d = x.shape
    tm, gb = min(ROW_TILE, s), ATTN_BLOCK
    n_pad = 3 * d + LANES
    w = jnp.pad(w_in, ((0, 0), (0, n_pad - w_in.shape[1]))).astype(BF16)
    bf = jnp.pad(b_f, (0, LANES - FOX_HEADS)).reshape(1, LANES).astype(F32)
    tri3, eq, ek, oq, ok, hs = _fox_aug_constants(gb)
    const = lambda shape: pl.BlockSpec(shape, lambda b, i: (0,) * len(shape))
    rows = lambda width: pl.BlockSpec((1, tm, width), lambda b, i: (b, i, 0))
    per_block = pl.BlockSpec((1, 1, tm // gb, LANES), lambda b, i: (b, i, 0, 0))
    return pl.pallas_call(
        functools.partial(_fox_inproj_kernel, tm=tm, gb=gb),
        grid=(bsz, s // tm),
        in_specs=[rows(d), pl.BlockSpec((1, 6, d), lambda b, i: (b, 0, 0)), const(w.shape),
                  const(bf.shape), const(tri3.shape), const(eq.shape), const(ek.shape),
                  const(oq.shape), const(ok.shape), const(hs.shape)],
        out_specs=[rows(d), rows(d), rows(d), rows(LANES), rows(LANES), per_block, per_block,
                   pl.BlockSpec((1, 1, 1, LANES), lambda b, i: (b, i, 0, 0))],
        out_shape=[jax.ShapeDtypeStruct((bsz, s, d), BF16)] * 3
        + [jax.ShapeDtypeStruct((bsz, s, LANES), BF16)] * 2
        + [jax.ShapeDtypeStruct((bsz, s // tm, tm // gb, LANES), F32)] * 2
        + [jax.ShapeDtypeStruct((bsz, s // tm, 1, LANES), F32)],
        scratch_shapes=[pltpu.VMEM((1, LANES), F32), pltpu.VMEM((1, LANES), F32)],
        compiler_params=_compiler_params(("arbitrary", "arbitrary")),
        name="fox_inproj",
    )(x, mod, w, bf, tri3, eq, ek, oq, ok, hs)


def _fox_attn_kernel(r_ref, qn_ref, kn_ref, q_ref, qaug_ref, k_ref, v_ref, kaug_ref, o_ref, *, blk, nblk):
    b, hp, i = pl.program_id(0), pl.program_id(1), pl.program_id(2)
    lane = lax.broadcasted_iota(jnp.int32, (1, LANES), 1)
    q = q_ref[0]
    qa = qaug_ref[0]
    zero = jnp.zeros((), BF16)
    nt = (((1,), (1,)), ((), ()))

    qcat, own, rbase, ubound, first = [], [], [], [], []
    for hh in range(2):
        h = 2 * hp + hh
        own.append((lane >= hh * FOX_HEAD_DIM) & (lane < (hh + 1) * FOX_HEAD_DIM))
        aug = (lane >= AUG_PER_HEAD * h) & (lane < AUG_PER_HEAD * (h + 1))
        qcat.append(jnp.concatenate([jnp.where(own[hh], q, zero), jnp.where(aug, qa, zero)], axis=1))
        base = (b * FOX_HEADS + h) * nblk
        rbase.append(base)
        ub = qn_ref[base + i] * kn_ref[b * FOX_HEADS + h]
        ubound.append(ub)
        thresh = -(SKIP_LOG_MARGIN + 2.0 * ub)
        r_i = r_ref[base + i]
        first.append(lax.fori_loop(
            0, i, lambda j, c, base=base, r_i=r_i, thresh=thresh:
            c + (r_i - r_ref[base + j + 1] < thresh).astype(jnp.int32), jnp.int32(0)))
    j_first = jnp.minimum(first[0], first[1])
    fast = 2.0 * jnp.maximum(ubound[0], ubound[1]) <= FAST_MAX_GAP

    def load_keys(j):
        start = pl.multiple_of(j * blk, blk)
        ks = k_ref[0, pl.ds(start, blk), :]
        ka = kaug_ref[0, pl.ds(start, blk), :]
        vs = v_ref[0, pl.ds(start, blk), :]
        return jnp.concatenate([ks, ka], axis=1), vs

    def causal_mask(z):
        rr = lax.broadcasted_iota(jnp.int32, (blk, blk), 0)
        cc = lax.broadcasted_iota(jnp.int32, (blk, blk), 1)
        return jnp.where(rr >= cc, z, -jnp.inf)

    @pl.when(fast)
    def _():
        ones_col = [jnp.broadcast_to(jnp.where(lane == hh, 1.0, 0.0), (blk, LANES)).astype(BF16)
                    for hh in range(2)]

        def tile(j, acc, diagonal):
            kcat, vs = load_keys(j)
            ps, vaug = [], []
            for hh in range(2):
                z = lax.dot_general(qcat[hh], kcat, nt, preferred_element_type=F32)
                z = z + (r_ref[rbase[hh] + i] - r_ref[rbase[hh] + j] - ubound[hh])
                if diagonal:
                    z = causal_mask(z)
                ps.append(jnp.exp(z).astype(BF16))
                vaug.append(jnp.concatenate([jnp.where(own[hh], vs, zero), ones_col[hh]], axis=1))
            return acc + jnp.dot(jnp.concatenate(ps, axis=1), jnp.concatenate(vaug, axis=0),
                                 preferred_element_type=F32)

        acc = lax.fori_loop(j_first, i, lambda j, c: tile(j, c, False), jnp.zeros((blk, 2 * LANES), F32))
        acc = tile(i, acc, True)
        l = jnp.where(own[0], acc[:, LANES:LANES + 1], acc[:, LANES + 1:LANES + 2])
        o_ref[0] = (acc[:, :LANES] / l).astype(o_ref.dtype)

    @pl.when(jnp.logical_not(fast))
    def _():
        def tile(j, carry, diagonal):
            kcat, vs = load_keys(j)
            out = []
            for hh in range(2):
                m, l, acc = carry[hh]
                z = lax.dot_general(qcat[hh], kcat, nt, preferred_element_type=F32)
                dij = r_ref[rbase[hh] + i] - r_ref[rbase[hh] + j]
                if diagonal:
                    z = causal_mask(z)
                m_new = jnp.maximum(m, jnp.max(z, axis=1, keepdims=True) + dij)
                p = jnp.exp(z - (m_new - dij))
                alpha = jnp.exp(m - m_new)
                l = alpha * l + jnp.sum(p, axis=1, keepdims=True)
                vm = jnp.where(own[hh], vs, zero)
                acc = alpha * acc + jnp.dot(p.astype(BF16), vm, preferred_element_type=F32)
                out.append((m_new, l, acc))
            return tuple(out)

        init = tuple((jnp.full((blk, 1), -jnp.inf, F32), jnp.zeros((blk, 1), F32),
                      jnp.zeros((blk, LANES), F32)) for _ in range(2))
        carry = lax.fori_loop(j_first, i, lambda j, c: tile(j, c, False), init)
        carry = tile(i, carry, True)
        (_, l0, a0), (_, l1, a1) = carry
        o_ref[0] = (a0 / l0 + a1 / l1).astype(o_ref.dtype)


def _fox_attention(q, k, v, qaug, kaug, rblk, qnblk, kntile):
    bsz, s, d = q.shape
    blk = ATTN_BLOCK
    nblk = s // blk
    flat = lambda a: a.reshape(bsz, nblk, LANES)[:, :, :FOX_HEADS].transpose(0, 2, 1).reshape(-1)
    knflat = kntile[:, -1, 0, :FOX_HEADS].reshape(-1)
    idx = lambda f: (lambda b, hp, i, r, qn, kn: f(b, hp, i))
    grid_spec = pltpu.PrefetchScalarGridSpec(
        num_scalar_prefetch=3,
        grid=(bsz, FOX_HEADS // 2, nblk),
        in_specs=[
            pl.BlockSpec((1, blk, LANES), idx(lambda b, hp, i: (b, i, hp))),
            pl.BlockSpec((1, blk, LANES), idx(lambda b, hp, i: (b, i, 0))),
            pl.BlockSpec((1, s, LANES), idx(lambda b, hp, i: (b, 0, hp))),
            pl.BlockSpec((1, s, LANES), idx(lambda b, hp, i: (b, 0, hp))),
            pl.BlockSpec((1, s, LANES), idx(lambda b, hp, i: (b, 0, 0))),
        ],
        out_specs=pl.BlockSpec((1, blk, LANES), idx(lambda b, hp, i: (b, i, hp))),
    )
    return pl.pallas_call(
        functools.partial(_fox_attn_kernel, blk=blk, nblk=nblk),
        grid_spec=grid_spec,
        out_shape=jax.ShapeDtypeStruct((bsz, s, d), BF16),
        compiler_params=_compiler_params(("arbitrary", "arbitrary", "arbitrary")),
        name="fox_attention",
    )(flat(rblk), flat(qnblk), knflat, q, qaug, k, v, kaug)


def _proj_ln_kernel(a_ref, w_ref, x_ref, mod_ref, g_ref, b_ref, o_ref, *, gate_row):
    y = jnp.dot(a_ref[0], w_ref[...], preferred_element_type=F32)
    r = RESIDUAL_ALPHA * x_ref[0] + (1.0 + mod_ref[0, gate_row:gate_row + 1, :]) * y
    o_ref[0] = _layer_norm(r, g_ref[...], b_ref[...])


def _proj_residual_ln(a, w, x, mod, ln_g, ln_b, gate_row):
    bsz, s, d = x.shape
    kdim = a.shape[-1]
    tm = min(ROW_TILE, s)
    return pl.pallas_call(
        functools.partial(_proj_ln_kernel, gate_row=gate_row),
        grid=(bsz, s // tm),
        in_specs=[
            pl.BlockSpec((1, tm, kdim), lambda b, i: (b, i, 0)),
            pl.BlockSpec((kdim, d), lambda b, i: (0, 0)),
            pl.BlockSpec((1, tm, d), lambda b, i: (b, i, 0)),
            pl.BlockSpec((1, 6, d), lambda b, i: (b, 0, 0)),
            pl.BlockSpec((1, d), lambda b, i: (0, 0)),
            pl.BlockSpec((1, d), lambda b, i: (0, 0)),
        ],
        out_specs=pl.BlockSpec((1, tm, d), lambda b, i: (b, i, 0)),
        out_shape=jax.ShapeDtypeStruct((bsz, s, d), F32),
        compiler_params=_compiler_params(("arbitrary", "arbitrary")),
        name="proj_residual_ln",
    )(a, w.astype(BF16), x, mod, ln_g.reshape(1, d), ln_b.reshape(1, d))


def _mlp_ln_kernel(x_ref, mod_ref, w1_ref, w2_ref, g_ref, b_ref, o_ref, *, ff_chunk):
    x = x_ref[0]
    u = (x * (1.0 + mod_ref[0, 4:5, :]) + mod_ref[0, 3:4, :]).astype(BF16)
    y = None
    for c in range(D_FF // ff_chunk):
        h = jnp.dot(u, w1_ref[:, c * ff_chunk:(c + 1) * ff_chunk], preferred_element_type=F32)
        h = jnp.square(jnp.maximum(h, 0.0)).astype(BF16)
        part = jnp.dot(h, w2_ref[c * ff_chunk:(c + 1) * ff_chunk, :], preferred_element_type=F32)
        y = part if y is None else y + part
    r = RESIDUAL_ALPHA * x + (1.0 + mod_ref[0, 5:6, :]) * y
    o_ref[0] = _layer_norm(r, g_ref[...], b_ref[...])


def _mlp_residual_ln(x, mod, w1, w2, ln_g, ln_b):
    bsz, s, d = x.shape
    tm = min(ROW_TILE, s)
    return pl.pallas_call(
        functools.partial(_mlp_ln_kernel, ff_chunk=1024),
        grid=(bsz, s // tm),
        in_specs=[
            pl.BlockSpec((1, tm, d), lambda b, i: (b, i, 0)),
            pl.BlockSpec((1, 6, d), lambda b, i: (b, 0, 0)),
            pl.BlockSpec((d, D_FF), lambda b, i: (0, 0)),
            pl.BlockSpec((D_FF, d), lambda b, i: (0, 0)),
            pl.BlockSpec((1, d), lambda b, i: (0, 0)),
            pl.BlockSpec((1, d), lambda b, i: (0, 0)),
        ],
        out_specs=pl.BlockSpec((1, tm, d), lambda b, i: (b, i, 0)),
        out_shape=jax.ShapeDtypeStruct((bsz, s, d), F32),
        compiler_params=_compiler_params(("arbitrary", "arbitrary")),
        name="mlp_residual_ln",
    )(x, mod, w1.astype(BF16), w2.astype(BF16), ln_g.reshape(1, d), ln_b.reshape(1, d))


def _ssd_inproj_kernel(x_ref, mod_ref, w_ref, z_ref, xbc_ref, dt_ref, *, chunk):
    u = (x_ref[0] * (1.0 + mod_ref[0, 1:2, :]) + mod_ref[0, 0:1, :]).astype(BF16)
    for c in range(SSM_D_INNER // chunk):
        z_ref[0, :, c * chunk:(c + 1) * chunk] = jnp.dot(
            u, w_ref[:, c * chunk:(c + 1) * chunk], preferred_element_type=F32)
    off = SSM_D_INNER
    for c in range(SSM_CONV_DIM // chunk):
        xbc_ref[0, :, c * chunk:(c + 1) * chunk] = jnp.dot(
            u, w_ref[:, off + c * chunk:off + (c + 1) * chunk], preferred_element_type=F32)
    off = SSM_D_INNER + SSM_CONV_DIM
    dt_ref[0] = jnp.dot(u, w_ref[:, off:off + LANES], preferred_element_type=F32)


def _ssd_inproj(x, mod, w_in):
    bsz, s, d = x.shape
    tm = min(ROW_TILE, s)
    n_pad = SSM_D_INNER + SSM_CONV_DIM + LANES
    w = jnp.pad(w_in, ((0, 0), (0, n_pad - w_in.shape[1]))).astype(BF16)
    rows = lambda width: pl.BlockSpec((1, tm, width), lambda b, i: (b, i, 0))
    return pl.pallas_call(
        functools.partial(_ssd_inproj_kernel, chunk=1024),
        grid=(bsz, s // tm),
        in_specs=[rows(d), pl.BlockSpec((1, 6, d), lambda b, i: (b, 0, 0)),
                  pl.BlockSpec(w.shape, lambda b, i: (0, 0))],
        out_specs=[rows(SSM_D_INNER), rows(SSM_CONV_DIM), rows(LANES)],
        out_shape=[jax.ShapeDtypeStruct((bsz, s, SSM_D_INNER), F32),
                   jax.ShapeDtypeStruct((bsz, s, SSM_CONV_DIM), F32),
                   jax.ShapeDtypeStruct((bsz, s, LANES), F32)],
        compiler_params=_compiler_params(("arbitrary", "arbitrary")),
        name="ssd_inproj",
    )(x, mod, w)


def _ssd_core_kernel(xbc_ref, z_ref, dt_ref, convw_ref, convb_ref, dtb_ref, alog_ref, dskip_ref,
                     normw_ref, tri_ref, ea_ref, eb_ref, oa_ref, ob_ref, xp_ref,
                     o_ref, ext_ref, state_ref):
    ln = SSM_CHUNK
    halo = SUBLANES

    @pl.when(pl.program_id(1) == 0)
    def _():
        ext_ref[0:halo, :] = jnp.zeros((halo, SSM_CONV_DIM), F32)
        state_ref[...] = jnp.zeros_like(state_ref)

    ext_ref[halo:halo + ln, :] = xbc_ref[0]
    conv = convb_ref[...]
    for kk in range(SSM_CONV):
        shift = halo - (SSM_CONV - 1) + kk
        conv = conv + ext_ref[shift:shift + ln, :] * convw_ref[kk:kk + 1, :]
    ext_ref[0:halo, :] = ext_ref[ln:ln + halo, :]
    xbc = conv * _sigmoid(conv)
    xs = xbc[:, :SSM_D_INNER]

    dt = _softplus(dt_ref[0] + dtb_ref[...])
    dta = dt * (-jnp.exp(alog_ref[...]))
    acs = jnp.dot(tri_ref[...], jnp.concatenate(_split3(dta), axis=0), preferred_element_type=F32)
    acs_last = acs[ln - 1:ln, :]
    eacs = jnp.exp(acs)
    dtdte = dt * jnp.exp(acs_last - acs)

    acat = jnp.concatenate(_split3(acs), axis=1)
    a_aug = (jnp.dot(acat, ea_ref[...], preferred_element_type=F32) + oa_ref[...]).astype(BF16)
    b_aug = (jnp.dot(acat, eb_ref[...], preferred_element_type=F32) + ob_ref[...]).astype(BF16)

    def expand(v):
        return jnp.dot(jnp.concatenate(_split3(v), axis=1), xp_ref[...], preferred_element_type=F32)

    xdt = (xs * expand(dt)).astype(BF16)
    xdte = (xs * expand(dtdte)).astype(BF16)
    eacs_x = expand(eacs)

    rr = lax.broadcasted_iota(jnp.int32, (ln, ln), 0)
    cc = lax.broadcasted_iota(jnp.int32, (ln, ln), 1)
    causal = rr >= cc
    lane_aug = lax.broadcasted_iota(jnp.int32, (1, 2 * LANES), 1)
    lane_grp = lax.broadcasted_iota(jnp.int32, (1, SSM_GROUP_WIDTH), 1)
    zero = jnp.zeros((), BF16)
    nt = (((1,), (1,)), ((), ()))
    tn = (((0,), (0,)), ((), ()))

    for g in range(SSM_GROUPS):
        gs = slice(g * SSM_GROUP_WIDTH, (g + 1) * SSM_GROUP_WIDTH)
        bg = xbc[:, SSM_D_INNER + g * SSM_STATE:SSM_D_INNER + (g + 1) * SSM_STATE].astype(BF16)
        cg = xbc[:, SSM_D_INNER + SSM_BC_WIDTH + g * SSM_STATE:
                 SSM_D_INNER + SSM_BC_WIDTH + (g + 1) * SSM_STATE].astype(BF16)
        cb = lax.dot_general(cg, bg, nt, preferred_element_type=F32)
        xg = xdt[:, gs]
        y = None
        for r in range(SSM_HEADS_PER_GROUP):
            h = g * SSM_HEADS_PER_GROUP + r
            sel = (lane_aug >= AUG_PER_HEAD * h) & (lane_aug < AUG_PER_HEAD * (h + 1))
            seg = lax.dot_general(jnp.where(sel, a_aug, zero), b_aug, nt, preferred_element_type=F32)
            decay = jnp.exp(jnp.where(causal, seg, -jnp.inf))
            own = (lane_grp >= r * SSM_HEAD_DIM) & (lane_grp < (r + 1) * SSM_HEAD_DIM)
            part = jnp.dot((cb * decay).astype(BF16), jnp.where(own, xg, zero),
                           preferred_element_type=F32)
            y = part if y is None else y + part
        prev = state_ref[g]
        y = y + jnp.dot(cg, prev.astype(BF16), preferred_element_type=F32) * eacs_x[:, gs]
        y = y + xs[:, gs] * dskip_ref[:, gs]
        new_states = lax.dot_general(bg, xdte[:, gs], tn, preferred_element_type=F32)
        state_ref[g] = prev * eacs_x[ln - 1:ln, gs] + new_states
        zg = z_ref[0, :, gs]
        gated = y * (zg * _sigmoid(zg))
        ms = jnp.mean(gated * gated, axis=-1, keepdims=True)
        o_ref[0, :, gs] = (gated * lax.rsqrt(ms + RMS_EPS) * normw_ref[:, gs]).astype(o_ref.dtype)


def _ssd_constants():
    ln = SSM_CHUNK
    t = jnp.arange(ln)
    tri = (t[:, None] >= t[None, :]).astype(BF16)
    tri3 = jnp.concatenate([tri] * SPLIT, axis=1)
    lane = jnp.arange(2 * LANES)
    row = jnp.arange(SPLIT * LANES)
    piece, head = row // LANES, row % LANES
    valid = head < SSM_HEADS
    ea = (valid[:, None] & (lane[None, :] == (AUG_PER_HEAD * head + piece)[:, None])).astype(BF16)
    eb = -(valid[:, None] & (lane[None, :] == (AUG_PER_HEAD * head + SPLIT + piece)[:, None])).astype(BF16)
    used = lane < AUG_PER_HEAD * SSM_HEADS
    oa = (used & (lane % AUG_PER_HEAD >= SPLIT)).astype(F32)[None, :]
    ob = (used & (lane % AUG_PER_HEAD < SPLIT)).astype(F32)[None, :]
    col = jnp.arange(SSM_D_INNER)
    xp = (valid[:, None] & (head[:, None] == (col // SSM_HEAD_DIM)[None, :])).astype(BF16)
    return tri3, ea, eb, oa, ob, xp


def _ssd_core(z, xbc, dt, conv_w, conv_b, dt_bias, a_log, d_skip, norm_w):
    bsz, s, _ = z.shape
    ln = SSM_CHUNK
    pad_heads = lambda v: jnp.pad(v.astype(F32), (0, LANES - SSM_HEADS)).reshape(1, LANES)
    consts = _ssd_constants()
    const = lambda a: pl.BlockSpec(a.shape, lambda b, i: (0,) * a.ndim)
    rows = lambda width: pl.BlockSpec((1, ln, width), lambda b, i: (b, i, 0))
    small = [conv_w.astype(F32), conv_b.reshape(1, -1).astype(F32), pad_heads(dt_bias), pad_heads(a_log),
             jnp.repeat(d_skip.astype(F32), SSM_HEAD_DIM).reshape(1, -1), norm_w.reshape(1, -1).astype(F32)]
    return pl.pallas_call(
        _ssd_core_kernel,
        grid=(bsz, s // ln),
        in_specs=[rows(SSM_CONV_DIM), rows(SSM_D_INNER), rows(LANES)]
        + [const(a) for a in small] + [const(a) for a in consts],
        out_specs=rows(SSM_D_INNER),
        out_shape=jax.ShapeDtypeStruct((bsz, s, SSM_D_INNER), BF16),
        scratch_shapes=[pltpu.VMEM((SUBLANES + ln, SSM_CONV_DIM), F32),
                        pltpu.VMEM((SSM_GROUPS, SSM_STATE, SSM_GROUP_WIDTH), F32)],
        compiler_params=_compiler_params(("arbitrary", "arbitrary")),
        name="ssd_core",
    )(xbc, z, dt, *small, *consts)


def kernel(x, c, ada_w, ada_b, ln_mix_g, ln_mix_b, ln_mlp_g, ln_mlp_b, mlp_w1, mlp_w2, fox_w_in, fox_b_f, fox_w_o, ssm_w_in, ssm_conv_w, ssm_conv_b, ssm_dt_bias, ssm_a_log, ssm_d, ssm_norm_w, ssm_w_out):
    mod = _ada_modulation(c, ada_w, ada_b)

    q, k, v, qaug, kaug, rblk, qnblk, kntile = _fox_inproj(x, mod[0], fox_w_in[0], fox_b_f[0])
    attn = _fox_attention(q, k, v, qaug, kaug, rblk, qnblk, kntile)
    x = _proj_residual_ln(attn, fox_w_o[0], x, mod[0], ln_mix_g[0], ln_mix_b[0], gate_row=2)
    x = _mlp_residual_ln(x, mod[0], mlp_w1[0], mlp_w2[0], ln_mlp_g[0], ln_mlp_b[0])

    z, xbc, dt = _ssd_inproj(x, mod[1], ssm_w_in[0])
    y = _ssd_core(z, xbc, dt, ssm_conv_w[0], ssm_conv_b[0], ssm_dt_bias[0], ssm_a_log[0],
                  ssm_d[0], ssm_norm_w[0])
    x = _proj_residual_ln(y, ssm_w_out[0], x, mod[1], ln_mix_g[1], ln_mix_b[1], gate_row=2)
    x = _mlp_residual_ln(x, mod[1], mlp_w1[1], mlp_w2[1], ln_mlp_g[1], ln_mlp_b[1])
    return x
```

```python
import functools

import jax
import jax.numpy as jnp
from jax import lax
from jax.experimental import pallas as pl
from jax.experimental.pallas import tpu as pltpu

F32 = jnp.float32
BF16 = jnp.bfloat16

D_MODEL = 1024
DEPTH = 2
FOX_HEADS = 16
FOX_HEAD_DIM = D_MODEL // FOX_HEADS
SSM_D_INNER = 2 * D_MODEL
SSM_HEAD_DIM = 64
SSM_HEADS = SSM_D_INNER // SSM_HEAD_DIM
SSM_GROUPS = 8
SSM_HEADS_PER_GROUP = SSM_HEADS // SSM_GROUPS
SSM_STATE = 128
SSM_CONV = 4
SSM_CHUNK = 128
SSM_GROUP_WIDTH = SSM_D_INNER // SSM_GROUPS
SSM_BC_WIDTH = SSM_GROUPS * SSM_STATE
SSM_CONV_DIM = SSM_D_INNER + 2 * SSM_BC_WIDTH
D_FF = 4 * D_MODEL
LN_EPS = 1e-5
RMS_EPS = 1e-5
RESIDUAL_ALPHA = (2.0 * DEPTH) ** 0.25

LANES = 128
SUBLANES = 8
VMEM_LIMIT_BYTES = 56 * 1024 * 1024

ATTN_BLOCK = 512
ROW_TILE = 512
AUG_PER_HEAD = 6
SPLIT = 3
NORM_SLACK = 1.01
SKIP_LOG_MARGIN = 32.0
FAST_MAX_GAP = 60.0


def _split3(v):
    hi = v.astype(BF16)
    r1 = v - hi.astype(F32)
    mid = r1.astype(BF16)
    lo = (r1 - mid.astype(F32)).astype(BF16)
    return hi, mid, lo


def _softplus(y):
    return jnp.maximum(y, 0.0) + jnp.log1p(jnp.exp(-jnp.abs(y)))


def _sigmoid(y):
    return 1.0 / (1.0 + jnp.exp(-y))


def _layer_norm(r, g, b):
    mu = jnp.mean(r, axis=-1, keepdims=True)
    cen = r - mu
    var = jnp.mean(cen * cen, axis=-1, keepdims=True)
    return cen * lax.rsqrt(var + LN_EPS) * g + b


def _compiler_params(semantics):
    return pltpu.CompilerParams(dimension_semantics=semantics, vmem_limit_bytes=VMEM_LIMIT_BYTES)


def _ada_kernel(c_ref, w_ref, b_ref, o_ref):
    c = c_ref[...]
    cond = c * _sigmoid(c)
    o_ref[0] = jnp.dot(cond, w_ref[0], precision=lax.Precision.HIGHEST,
                       preferred_element_type=F32) + b_ref[0]


def _ada_modulation(c, ada_w, ada_b):
    depth, d, n = ada_w.shape
    bsz = c.shape[0]
    tn = 1536
    out = pl.pallas_call(
        _ada_kernel,
        grid=(depth, n // tn),
        in_specs=[
            pl.BlockSpec((bsz, d), lambda i, j: (0, 0)),
            pl.BlockSpec((1, d, tn), lambda i, j: (i, 0, j)),
            pl.BlockSpec((1, 1, tn), lambda i, j: (i, 0, j)),
        ],
        out_specs=pl.BlockSpec((1, bsz, tn), lambda i, j: (i, 0, j)),
        out_shape=jax.ShapeDtypeStruct((depth, bsz, n), F32),
        compiler_params=_compiler_params(("arbitrary", "arbitrary")),
        name="ada_modulation",
    )(c, ada_w, ada_b.reshape(depth, 1, n))
    return out.reshape(depth, bsz, 6, d)


def _fox_inproj_kernel(x_ref, mod_ref, w_ref, bf_ref, tri_ref, eq_ref, ek_ref, oq_ref, ok_ref, hs_ref,
                       q_ref, k_ref, v_ref, qaug_ref, kaug_ref, rblk_ref, qn_ref, kn_ref,
                       carry_ref, kmax_ref, *, tm, gb):
    @pl.when(pl.program_id(1) == 0)
    def _():
        carry_ref[...] = jnp.zeros_like(carry_ref)
        kmax_ref[...] = jnp.zeros_like(kmax_ref)

    x = x_ref[0]
    u = (x * (1.0 + mod_ref[0, 1:2, :]) + mod_ref[0, 0:1, :]).astype(BF16)
    d = D_MODEL
    qb = (jnp.dot(u, w_ref[:, 0:d], preferred_element_type=F32) * (FOX_HEAD_DIM ** -0.5)).astype(BF16)
    kb = jnp.dot(u, w_ref[:, d:2 * d], preferred_element_type=F32).astype(BF16)
    q_ref[0] = qb
    k_ref[0] = kb

    def head_norms(t):
        t = t.astype(F32)
        return jnp.sqrt(jnp.dot((t * t).astype(BF16), hs_ref[...], preferred_element_type=F32)) * NORM_SLACK

    qnorm = head_norms(qb)
    kmax_ref[...] = jnp.maximum(kmax_ref[...], jnp.max(head_norms(kb), axis=0, keepdims=True))
    kn_ref[0, 0] = kmax_ref[...]
    v_ref[0] = jnp.dot(u, w_ref[:, 2 * d:3 * d], preferred_element_type=F32).astype(BF16)
    f = jnp.dot(u, w_ref[:, 3 * d:3 * d + LANES], preferred_element_type=F32) + bf_ref[...]
    logf = jnp.minimum(f, 0.0) - jnp.log1p(jnp.exp(-jnp.abs(f)))

    for g in range(tm // gb):
        lf = logf[g * gb:(g + 1) * gb]
        rel = jnp.dot(tri_ref[...], jnp.concatenate(_split3(lf), axis=0), preferred_element_type=F32)
        rcat = jnp.concatenate(_split3(rel), axis=1)
        qaug_ref[0, g * gb:(g + 1) * gb, :] = (
            jnp.dot(rcat, eq_ref[...], preferred_element_type=F32) + oq_ref[...]).astype(BF16)
        kaug_ref[0, g * gb:(g + 1) * gb, :] = (
            jnp.dot(rcat, ek_ref[...], preferred_element_type=F32) + ok_ref[...]).astype(BF16)
        rblk_ref[0, 0, g:g + 1, :] = carry_ref[...]
        qn_ref[0, 0, g:g + 1, :] = jnp.max(qnorm[g * gb:(g + 1) * gb], axis=0, keepdims=True)
        carry_ref[...] = carry_ref[...] + rel[gb - 1:gb, :]


def _fox_aug_constants(gb):
    lane = jnp.arange(LANES)
    row = jnp.arange(SPLIT * LANES)
    piece, head = row // LANES, row % LANES
    valid = head < FOX_HEADS
    eq = (valid[:, None] & (lane[None, :] == (AUG_PER_HEAD * head + piece)[:, None])).astype(BF16)
    ek = -(valid[:, None] & (lane[None, :] == (AUG_PER_HEAD * head + SPLIT + piece)[:, None])).astype(BF16)
    used = lane < AUG_PER_HEAD * FOX_HEADS
    oq = (used & (lane % AUG_PER_HEAD >= SPLIT)).astype(F32)[None, :]
    ok = (used & (lane % AUG_PER_HEAD < SPLIT)).astype(F32)[None, :]
    t = jnp.arange(gb)
    tri = (t[:, None] >= t[None, :]).astype(BF16)
    tri3 = jnp.concatenate([tri] * SPLIT, axis=1)
    hs = (jnp.arange(D_MODEL)[:, None] // FOX_HEAD_DIM == lane[None, :]).astype(BF16)
    return tri3, eq, ek, oq, ok, hs


def _fox_inproj(x, mod, w_in, b_f):
    bsz, s, d = x.shape
    tm, gb = min(ROW_TILE, s), ATTN_BLOCK
    n_pad = 3 * d + LANES
    w = jnp.pad(w_in, ((0, 0), (0, n_pad - w_in.shape[1]))).astype(BF16)
    bf = jnp.pad(b_f, (0, LANES - FOX_HEADS)).reshape(1, LANES).astype(F32)
    tri3, eq, ek, oq, ok, hs = _fox_aug_constants(gb)
    const = lambda shape: pl.BlockSpec(shape, lambda b, i: (0,) * len(shape))
    rows = lambda width: pl.BlockSpec((1, tm, width), lambda b, i: (b, i, 0))
    per_block = pl.BlockSpec((1, 1, tm // gb, LANES), lambda b, i: (b, i, 0, 0))
    return pl.pallas_call(
        functools.partial(_fox_inproj_kernel, tm=tm, gb=gb),
        grid=(bsz, s // tm),
        in_specs=[rows(d), pl.BlockSpec((1, 6, d), lambda b, i: (b, 0, 0)), const(w.shape),
                  const(bf.shape), const(tri3.shape), const(eq.shape), const(ek.shape),
                  const(oq.shape), const(ok.shape), const(hs.shape)],
        out_specs=[rows(d), rows(d), rows(d), rows(LANES), rows(LANES), per_block, per_block,
                   pl.BlockSpec((1, 1, 1, LANES), lambda b, i: (b, i, 0, 0))],
        out_shape=[jax.ShapeDtypeStruct((bsz, s, d), BF16)] * 3
        + [jax.ShapeDtypeStruct((bsz, s, LANES), BF16)] * 2
        + [jax.ShapeDtypeStruct((bsz, s // tm, tm // gb, LANES), F32)] * 2
        + [jax.ShapeDtypeStruct((bsz, s // tm, 1, LANES), F32)],
        scratch_shapes=[pltpu.VMEM((1, LANES), F32), pltpu.VMEM((1, LANES), F32)],
        compiler_params=_compiler_params(("arbitrary", "arbitrary")),
        name="fox_inproj",
    )(x, mod, w, bf, tri3, eq, ek, oq, ok, hs)


def _fox_attn_kernel(r_ref, qn_ref, kn_ref, q_ref, qaug_ref, k_ref, v_ref, kaug_ref, o_ref, *, blk, nblk):
    b, hp, i = pl.program_id(0), pl.program_id(1), pl.program_id(2)
    lane = lax.broadcasted_iota(jnp.int32, (1, LANES), 1)
    q = q_ref[0]
    qa = qaug_ref[0]
    zero = jnp.zeros((), BF16)
    nt = (((1,), (1,)), ((), ()))

    qcat, own, rbase, ubound, first = [], [], [], [], []
    for hh in range(2):
        h = 2 * hp + hh
        own.append((lane >= hh * FOX_HEAD_DIM) & (lane < (hh + 1) * FOX_HEAD_DIM))
        aug = (lane >= AUG_PER_HEAD * h) & (lane < AUG_PER_HEAD * (h + 1))
        qcat.append(jnp.concatenate([jnp.where(own[hh], q, zero), jnp.where(aug, qa, zero)], axis=1))
        base = (b * FOX_HEADS + h) * nblk
        rbase.append(base)
        ub = qn_ref[base + i] * kn_ref[b * FOX_HEADS + h]
        ubound.append(ub)
        thresh = -(SKIP_LOG_MARGIN + 2.0 * ub)
        r_i = r_ref[base + i]
        first.append(lax.while_loop(
            lambda j, base=base, r_i=r_i, thresh=thresh: (j > 0) & (r_i - r_ref[base + j] >= thresh),
            lambda j: j - 1, i))
    j_first = jnp.minimum(first[0], first[1])
    fast = 2.0 * jnp.maximum(ubound[0], ubound[1]) <= FAST_MAX_GAP

    def block_shift(hh, j):
        return r_ref[rbase[hh] + i] - r_ref[rbase[hh] + j]

    def causal_mask(z):
        rr = lax.broadcasted_iota(jnp.int32, (blk, blk), 0)
        cc = lax.broadcasted_iota(jnp.int32, (blk, blk), 1)
        return jnp.where(rr >= cc, z, -jnp.inf)

    def key_slice(ref, j):
        return ref[0, pl.ds(pl.multiple_of(j * blk, blk), blk), :]

    def key_operand(j):
        return jnp.concatenate([key_slice(k_ref, j), key_slice(kaug_ref, j)], axis=1)

    @pl.when(fast)
    def _():
        ones_col = [jnp.broadcast_to(jnp.where(lane == hh, 1.0, 0.0), (blk, LANES)).astype(BF16)
                    for hh in range(2)]

        def tile(j, diagonal):
            kcat = key_operand(j)
            vs = key_slice(v_ref, j)
            out = None
            for hh in range(2):
                z = lax.dot_general(qcat[hh], kcat, nt, preferred_element_type=F32)
                z = z + (block_shift(hh, j) - ubound[hh])
                if diagonal:
                    z = causal_mask(z)
                vaug = jnp.concatenate([jnp.where(own[hh], vs, zero), ones_col[hh]], axis=1)
                part = jnp.dot(jnp.exp(z).astype(BF16), vaug, preferred_element_type=F32)
                out = part if out is None else out + part
            return out

        def finish(acc):
            l = jnp.where(own[0], acc[:, LANES:LANES + 1], acc[:, LANES + 1:LANES + 2])
            o_ref[0] = (acc[:, :LANES] / l).astype(o_ref.dtype)

        @pl.when(j_first < i)
        def _():
            acc = lax.fori_loop(j_first, i - 1, lambda j, a: a + tile(j, False),
                                jnp.zeros((blk, 2 * LANES), F32))
            finish(acc + tile(i - 1, False) + tile(i, True))

        @pl.when(j_first >= i)
        def _():
            finish(tile(i, True))

    @pl.when(jnp.logical_not(fast))
    def _():
        def tile(j, carry, diagonal):
            kcat = key_operand(j)
            vs = key_slice(v_ref, j)
            out = []
            for hh in range(2):
                m, l, acc = carry[hh]
                z = lax.dot_general(qcat[hh], kcat, nt, preferred_element_type=F32)
                dij = block_shift(hh, j)
                if diagonal:
                    z = causal_mask(z)
                m_new = jnp.maximum(m, jnp.max(z, axis=1, keepdims=True) + dij)
                p = jnp.exp(z - (m_new - dij))
                alpha = jnp.exp(m - m_new)
                l = alpha * l + jnp.sum(p, axis=1, keepdims=True)
                vm = jnp.where(own[hh], vs, zero)
                acc = alpha * acc + jnp.dot(p.astype(BF16), vm, preferred_element_type=F32)
                out.append((m_new, l, acc))
            return tuple(out)

        init = tuple((jnp.full((blk, 1), -jnp.inf, F32), jnp.zeros((blk, 1), F32),
                      jnp.zeros((blk, LANES), F32)) for _ in range(2))
        carry = lax.fori_loop(j_first, i, lambda j, c: tile(j, c, False), init)
        carry = tile(i, carry, True)
        (_, l0, a0), (_, l1, a1) = carry
        o_ref[0] = (a0 / l0 + a1 / l1).astype(o_ref.dtype)


def _fox_attention(q, k, v, qaug, kaug, rblk, qnblk, kntile):
    bsz, s, d = q.shape
    blk = ATTN_BLOCK
    nblk = s // blk
    flat = lambda a: a.reshape(bsz, nblk, LANES)[:, :, :FOX_HEADS].transpose(0, 2, 1).reshape(-1)
    knflat = kntile[:, -1, 0, :FOX_HEADS].reshape(-1)
    idx = lambda f: (lambda b, hp, i, r, qn, kn: f(b, hp, i))
    grid_spec = pltpu.PrefetchScalarGridSpec(
        num_scalar_prefetch=3,
        grid=(bsz, FOX_HEADS // 2, nblk),
        in_specs=[
            pl.BlockSpec((1, blk, LANES), idx(lambda b, hp, i: (b, i, hp))),
            pl.BlockSpec((1, blk, LANES), idx(lambda b, hp, i: (b, i, 0))),
            pl.BlockSpec((1, s, LANES), idx(lambda b, hp, i: (b, 0, hp))),
            pl.BlockSpec((1, s, LANES), idx(lambda b, hp, i: (b, 0, hp))),
            pl.BlockSpec((1, s, LANES), idx(lambda b, hp, i: (b, 0, 0))),
        ],
        out_specs=pl.BlockSpec((1, blk, LANES), idx(lambda b, hp, i: (b, i, hp))),
    )
    return pl.pallas_call(
        functools.partial(_fox_attn_kernel, blk=blk, nblk=nblk),
        grid_spec=grid_spec,
        out_shape=jax.ShapeDtypeStruct((bsz, s, d), BF16),
        compiler_params=_compiler_params(("arbitrary", "arbitrary", "arbitrary")),
        name="fox_attention",
    )(flat(rblk), flat(qnblk), knflat, q, qaug, k, v, kaug)


def _proj_ln_kernel(a_ref, w_ref, x_ref, mod_ref, g_ref, b_ref, o_ref, *, gate_row):
    y = jnp.dot(a_ref[0], w_ref[...], preferred_element_type=F32)
    r = RESIDUAL_ALPHA * x_ref[0] + (1.0 + mod_ref[0, gate_row:gate_row + 1, :]) * y
    o_ref[0] = _layer_norm(r, g_ref[...], b_ref[...])


def _proj_residual_ln(a, w, x, mod, ln_g, ln_b, gate_row):
    bsz, s, d = x.shape
    kdim = a.shape[-1]
    tm = min(ROW_TILE, s)
    return pl.pallas_call(
        functools.partial(_proj_ln_kernel, gate_row=gate_row),
        grid=(bsz, s // tm),
        in_specs=[
            pl.BlockSpec((1, tm, kdim), lambda b, i: (b, i, 0)),
            pl.BlockSpec((kdim, d), lambda b, i: (0, 0)),
            pl.BlockSpec((1, tm, d), lambda b, i: (b, i, 0)),
            pl.BlockSpec((1, 6, d), lambda b, i: (b, 0, 0)),
            pl.BlockSpec((1, d), lambda b, i: (0, 0)),
            pl.BlockSpec((1, d), lambda b, i: (0, 0)),
        ],
        out_specs=pl.BlockSpec((1, tm, d), lambda b, i: (b, i, 0)),
        out_shape=jax.ShapeDtypeStruct((bsz, s, d), F32),
        compiler_params=_compiler_params(("arbitrary", "arbitrary")),
        name="proj_residual_ln",
    )(a, w.astype(BF16), x, mod, ln_g.reshape(1, d), ln_b.reshape(1, d))


def _mlp_ln_kernel(x_ref, mod_ref, w1_ref, w2_ref, g_ref, b_ref, o_ref, *, ff_chunk):
    x = x_ref[0]
    u = (x * (1.0 + mod_ref[0, 4:5, :]) + mod_ref[0, 3:4, :]).astype(BF16)
    y = None
    for c in range(D_FF // ff_chunk):
        h = jnp.dot(u, w1_ref[:, c * ff_chunk:(c + 1) * ff_chunk], preferred_element_type=F32)
        h = jnp.square(jnp.maximum(h, 0.0)).astype(BF16)
        part = jnp.dot(h, w2_ref[c * ff_chunk:(c + 1) * ff_chunk, :], preferred_element_type=F32)
        y = part if y is None else y + part
    r = RESIDUAL_ALPHA * x + (1.0 + mod_ref[0, 5:6, :]) * y
    o_ref[0] = _layer_norm(r, g_ref[...], b_ref[...])


def _mlp_residual_ln(x, mod, w1, w2, ln_g, ln_b):
    bsz, s, d = x.shape
    tm = min(ROW_TILE, s)
    return pl.pallas_call(
        functools.partial(_mlp_ln_kernel, ff_chunk=1024),
        grid=(bsz, s // tm),
        in_specs=[
            pl.BlockSpec((1, tm, d), lambda b, i: (b, i, 0)),
            pl.BlockSpec((1, 6, d), lambda b, i: (b, 0, 0)),
            pl.BlockSpec((d, D_FF), lambda b, i: (0, 0)),
            pl.BlockSpec((D_FF, d), lambda b, i: (0, 0)),
            pl.BlockSpec((1, d), lambda b, i: (0, 0)),
            pl.BlockSpec((1, d), lambda b, i: (0, 0)),
        ],
        out_specs=pl.BlockSpec((1, tm, d), lambda b, i: (b, i, 0)),
        out_shape=jax.ShapeDtypeStruct((bsz, s, d), F32),
        compiler_params=_compiler_params(("arbitrary", "arbitrary")),
        name="mlp_residual_ln",
    )(x, mod, w1.astype(BF16), w2.astype(BF16), ln_g.reshape(1, d), ln_b.reshape(1, d))


def _ssd_inproj_kernel(x_ref, mod_ref, w_ref, z_ref, xbc_ref, dt_ref, *, chunk):
    u = (x_ref[0] * (1.0 + mod_ref[0, 1:2, :]) + mod_ref[0, 0:1, :]).astype(BF16)
    for c in range(SSM_D_INNER // chunk):
        z_ref[0, :, c * chunk:(c + 1) * chunk] = jnp.dot(
            u, w_ref[:, c * chunk:(c + 1) * chunk], preferred_element_type=F32)
    off = SSM_D_INNER
    for c in range(SSM_CONV_DIM // chunk):
        xbc_ref[0, :, c * chunk:(c + 1) * chunk] = jnp.dot(
            u, w_ref[:, off + c * chunk:off + (c + 1) * chunk], preferred_element_type=F32)
    off = SSM_D_INNER + SSM_CONV_DIM
    dt_ref[0] = jnp.dot(u, w_ref[:, off:off + LANES], preferred_element_type=F32)


def _ssd_inproj(x, mod, w_in):
    bsz, s, d = x.shape
    tm = min(ROW_TILE, s)
    n_pad = SSM_D_INNER + SSM_CONV_DIM + LANES
    w = jnp.pad(w_in, ((0, 0), (0, n_pad - w_in.shape[1]))).astype(BF16)
    rows = lambda width: pl.BlockSpec((1, tm, width), lambda b, i: (b, i, 0))
    return pl.pallas_call(
        functools.partial(_ssd_inproj_kernel, chunk=1024),
        grid=(bsz, s // tm),
        in_specs=[rows(d), pl.BlockSpec((1, 6, d), lambda b, i: (b, 0, 0)),
                  pl.BlockSpec(w.shape, lambda b, i: (0, 0))],
        out_specs=[rows(SSM_D_INNER), rows(SSM_CONV_DIM), rows(LANES)],
        out_shape=[jax.ShapeDtypeStruct((bsz, s, SSM_D_INNER), F32),
                   jax.ShapeDtypeStruct((bsz, s, SSM_CONV_DIM), F32),
                   jax.ShapeDtypeStruct((bsz, s, LANES), F32)],
        compiler_params=_compiler_params(("arbitrary", "arbitrary")),
        name="ssd_inproj",
    )(x, mod, w)


def _ssd_core_kernel(xbc_ref, z_ref, dt_ref, convw_ref, convb_ref, dtb_ref, alog_ref, dskip_ref,
                     normw_ref, tri_ref, ea_ref, eb_ref, oa_ref, ob_ref, xp_ref,
                     o_ref, ext_ref, state_ref):
    ln = SSM_CHUNK
    halo = SUBLANES

    @pl.when(pl.program_id(1) == 0)
    def _():
        ext_ref[0:halo, :] = jnp.zeros((halo, SSM_CONV_DIM), F32)
        state_ref[...] = jnp.zeros_like(state_ref)

    ext_ref[halo:halo + ln, :] = xbc_ref[0]
    conv = convb_ref[...]
    for kk in range(SSM_CONV):
        shift = halo - (SSM_CONV - 1) + kk
        conv = conv + ext_ref[shift:shift + ln, :] * convw_ref[kk:kk + 1, :]
    ext_ref[0:halo, :] = ext_ref[ln:ln + halo, :]
    xbc = conv * _sigmoid(conv)
    xs = xbc[:, :SSM_D_INNER]

    dt = _softplus(dt_ref[0] + dtb_ref[...])
    dta = dt * (-jnp.exp(alog_ref[...]))
    acs = jnp.dot(tri_ref[...], jnp.concatenate(_split3(dta), axis=0), preferred_element_type=F32)
    acs_last = acs[ln - 1:ln, :]
    eacs = jnp.exp(acs)
    dtdte = dt * jnp.exp(acs_last - acs)

    acat = jnp.concatenate(_split3(acs), axis=1)
    a_aug = (jnp.dot(acat, ea_ref[...], preferred_element_type=F32) + oa_ref[...]).astype(BF16)
    b_aug = (jnp.dot(acat, eb_ref[...], preferred_element_type=F32) + ob_ref[...]).astype(BF16)

    def expand(v):
        return jnp.dot(jnp.concatenate(_split3(v), axis=1), xp_ref[...], preferred_element_type=F32)

    xdt = (xs * expand(dt)).astype(BF16)
    xdte = (xs * expand(dtdte)).astype(BF16)
    eacs_x = expand(eacs)

    rr = lax.broadcasted_iota(jnp.int32, (ln, ln), 0)
    cc = lax.broadcasted_iota(jnp.int32, (ln, ln), 1)
    causal = rr >= cc
    lane_aug = lax.broadcasted_iota(jnp.int32, (1, 2 * LANES), 1)
    lane_grp = lax.broadcasted_iota(jnp.int32, (1, SSM_GROUP_WIDTH), 1)
    zero = jnp.zeros((), BF16)
    nt = (((1,), (1,)), ((), ()))
    tn = (((0,), (0,)), ((), ()))

    for g in range(SSM_GROUPS):
        gs = slice(g * SSM_GROUP_WIDTH, (g + 1) * SSM_GROUP_WIDTH)
        bg = xbc[:, SSM_D_INNER + g * SSM_STATE:SSM_D_INNER + (g + 1) * SSM_STATE].astype(BF16)
        cg = xbc[:, SSM_D_INNER + SSM_BC_WIDTH + g * SSM_STATE:
                 SSM_D_INNER + SSM_BC_WIDTH + (g + 1) * SSM_STATE].astype(BF16)
        cb = lax.dot_general(cg, bg, nt, preferred_element_type=F32)
        xg = xdt[:, gs]
        y = None
        for r in range(SSM_HEADS_PER_GROUP):
            h = g * SSM_HEADS_PER_GROUP + r
            sel = (lane_aug >= AUG_PER_HEAD * h) & (lane_aug < AUG_PER_HEAD * (h + 1))
            seg = lax.dot_general(jnp.where(sel, a_aug, zero), b_aug, nt, preferred_element_type=F32)
            decay = jnp.exp(jnp.where(causal, seg, -jnp.inf))
            own = (lane_grp >= r * SSM_HEAD_DIM) & (lane_grp < (r + 1) * SSM_HEAD_DIM)
            part = jnp.dot((cb * decay).astype(BF16), jnp.where(own, xg, zero),
                           preferred_element_type=F32)
            y = part if y is None else y + part
        prev = state_ref[g]
        y = y + jnp.dot(cg, prev.astype(BF16), preferred_element_type=F32) * eacs_x[:, gs]
        y = y + xs[:, gs] * dskip_ref[:, gs]
        new_states = lax.dot_general(bg, xdte[:, gs], tn, preferred_element_type=F32)
        state_ref[g] = prev * eacs_x[ln - 1:ln, gs] + new_states
        zg = z_ref[0, :, gs]
        gated = y * (zg * _sigmoid(zg))
        ms = jnp.mean(gated * gated, axis=-1, keepdims=True)
        o_ref[0, :, gs] = (gated * lax.rsqrt(ms + RMS_EPS) * normw_ref[:, gs]).astype(o_ref.dtype)


def _ssd_constants():
    ln = SSM_CHUNK
    t = jnp.arange(ln)
    tri = (t[:, None] >= t[None, :]).astype(BF16)
    tri3 = jnp.concatenate([tri] * SPLIT, axis=1)
    lane = jnp.arange(2 * LANES)
    row = jnp.arange(SPLIT * LANES)
    piece, head = row // LANES, row % LANES
    valid = head < SSM_HEADS
    ea = (valid[:, None] & (lane[None, :] == (AUG_PER_HEAD * head + piece)[:, None])).astype(BF16)
    eb = -(valid[:, None] & (lane[None, :] == (AUG_PER_HEAD * head + SPLIT + piece)[:, None])).astype(BF16)
    used = lane < AUG_PER_HEAD * SSM_HEADS
    oa = (used & (lane % AUG_PER_HEAD >= SPLIT)).astype(F32)[None, :]
    ob = (used & (lane % AUG_PER_HEAD < SPLIT)).astype(F32)[None, :]
    col = jnp.arange(SSM_D_INNER)
    xp = (valid[:, None] & (head[:, None] == (col // SSM_HEAD_DIM)[None, :])).astype(BF16)
    return tri3, ea, eb, oa, ob, xp


def _ssd_core(z, xbc, dt, conv_w, conv_b, dt_bias, a_log, d_skip, norm_w):
    bsz, s, _ = z.shape
    ln = SSM_CHUNK
    pad_heads = lambda v: jnp.pad(v.astype(F32), (0, LANES - SSM_HEADS)).reshape(1, LANES)
    consts = _ssd_constants()
    const = lambda a: pl.BlockSpec(a.shape, lambda b, i: (0,) * a.ndim)
    rows = lambda width: pl.BlockSpec((1, ln, width), lambda b, i: (b, i, 0))
    small = [conv_w.astype(F32), conv_b.reshape(1, -1).astype(F32), pad_heads(dt_bias), pad_heads(a_log),
             jnp.repeat(d_skip.astype(F32), SSM_HEAD_DIM).reshape(1, -1), norm_w.reshape(1, -1).astype(F32)]
    return pl.pallas_call(
        _ssd_core_kernel,
        grid=(bsz, s // ln),
        in_specs=[rows(SSM_CONV_DIM), rows(SSM_D_INNER), rows(LANES)]
        + [const(a) for a in small] + [const(a) for a in consts],
        out_specs=rows(SSM_D_INNER),
        out_shape=jax.ShapeDtypeStruct((bsz, s, SSM_D_INNER), BF16),
        scratch_shapes=[pltpu.VMEM((SUBLANES + ln, SSM_CONV_DIM), F32),
                        pltpu.VMEM((SSM_GROUPS, SSM_STATE, SSM_GROUP_WIDTH), F32)],
        compiler_params=_compiler_params(("arbitrary", "arbitrary")),
        name="ssd_core",
    )(xbc, z, dt, *small, *consts)


def kernel(x, c, ada_w, ada_b, ln_mix_g, ln_mix_b, ln_mlp_g, ln_mlp_b, mlp_w1, mlp_w2, fox_w_in, fox_b_f, fox_w_o, ssm_w_in, ssm_conv_w, ssm_conv_b, ssm_dt_bias, ssm_a_log, ssm_d, ssm_norm_w, ssm_w_out):
    mod = _ada_modulation(c, ada_w, ada_b)

    q, k, v, qaug, kaug, rblk, qnblk, kntile = _fox_inproj(x, mod[0], fox_w_in[0], fox_b_f[0])
    attn = _fox_attention(q, k, v, qaug, kaug, rblk, qnblk, kntile)
    x = _proj_residual_ln(attn, fox_w_o[0], x, mod[0], ln_mix_g[0], ln_mix_b[0], gate_row=2)
    x = _mlp_residual_ln(x, mod[0], mlp_w1[0], mlp_w2[0], ln_mlp_g[0], ln_mlp_b[0])

    z, xbc, dt = _ssd_inproj(x, mod[1], ssm_w_in[0])
    y = _ssd_core(z, xbc, dt, ssm_conv_w[0], ssm_conv_b[0], ssm_dt_bias[0], ssm_a_log[0],
                  ssm_d[0], ssm_norm_w[0])
    x = _proj_residual_ln(y, ssm_w_out[0], x, mod[1], ln_mix_g[1], ln_mix_b[1], gate_row=2)
    x = _mlp_residual_ln(x, mod[1], mlp_w1[1], mlp_w2[1], ln_mlp_g[1], ln_mlp_b[1])
    return x
```

```python
import functools

import jax
import jax.numpy as jnp
from jax import lax
from jax.experimental import pallas as pl
from jax.experimental.pallas import tpu as pltpu

F32 = jnp.float32
BF16 = jnp.bfloat16

D_MODEL = 1024
DEPTH = 2
FOX_HEADS = 16
FOX_HEAD_DIM = D_MODEL // FOX_HEADS
SSM_D_INNER = 2 * D_MODEL
SSM_HEAD_DIM = 64
SSM_HEADS = SSM_D_INNER // SSM_HEAD_DIM
SSM_GROUPS = 8
SSM_HEADS_PER_GROUP = SSM_HEADS // SSM_GROUPS
SSM_STATE = 128
SSM_CONV = 4
SSM_CHUNK = 128
SSM_GROUP_WIDTH = SSM_D_INNER // SSM_GROUPS
SSM_BC_WIDTH = SSM_GROUPS * SSM_STATE
SSM_CONV_DIM = SSM_D_INNER + 2 * SSM_BC_WIDTH
D_FF = 4 * D_MODEL
LN_EPS = 1e-5
RMS_EPS = 1e-5
RESIDUAL_ALPHA = (2.0 * DEPTH) ** 0.25

LANES = 128
SUBLANES = 8
VMEM_LIMIT_BYTES = 56 * 1024 * 1024

ATTN_BLOCK = 512
ROW_TILE = 512
SSD_CHUNKS_PER_STEP = 2
AUG_PER_HEAD = 6
SPLIT = 3
NORM_SLACK = 1.01
SKIP_LOG_MARGIN = 32.0
FAST_MAX_GAP = 60.0


def _split3(v):
    hi = v.astype(BF16)
    r1 = v - hi.astype(F32)
    mid = r1.astype(BF16)
    lo = (r1 - mid.astype(F32)).astype(BF16)
    return hi, mid, lo


def _softplus(y):
    return jnp.maximum(y, 0.0) + jnp.log1p(jnp.exp(-jnp.abs(y)))


def _silu(y):
    h = 0.5 * y
    return h + h * jnp.tanh(h)


def _layer_norm(r, g, b):
    mu = jnp.mean(r, axis=-1, keepdims=True)
    cen = r - mu
    var = jnp.mean(cen * cen, axis=-1, keepdims=True)
    return cen * lax.rsqrt(var + LN_EPS) * g + b


def _compiler_params(semantics):
    return pltpu.CompilerParams(dimension_semantics=semantics, vmem_limit_bytes=VMEM_LIMIT_BYTES)


def _ada_kernel(c_ref, w_ref, b_ref, o_ref):
    c = c_ref[...]
    cond = _silu(c)
    o_ref[0] = jnp.dot(cond, w_ref[0], precision=lax.Precision.HIGHEST,
                       preferred_element_type=F32) + b_ref[0]


def _ada_modulation(c, ada_w, ada_b):
    depth, d, n = ada_w.shape
    bsz = c.shape[0]
    tn = 1536
    out = pl.pallas_call(
        _ada_kernel,
        grid=(depth, n // tn),
        in_specs=[
            pl.BlockSpec((bsz, d), lambda i, j: (0, 0)),
            pl.BlockSpec((1, d, tn), lambda i, j: (i, 0, j)),
            pl.BlockSpec((1, 1, tn), lambda i, j: (i, 0, j)),
        ],
        out_specs=pl.BlockSpec((1, bsz, tn), lambda i, j: (i, 0, j)),
        out_shape=jax.ShapeDtypeStruct((depth, bsz, n), F32),
        compiler_params=_compiler_params(("arbitrary", "arbitrary")),
        name="ada_modulation",
    )(c, ada_w, ada_b.reshape(depth, 1, n))
    return out.reshape(depth, bsz, 6, d)


def _fox_inproj_kernel(x_ref, mod_ref, w_ref, bf_ref, tri_ref, eq_ref, ek_ref, oq_ref, ok_ref, hs_ref,
                       q_ref, k_ref, v_ref, qaug_ref, kaug_ref, rblk_ref, qn_ref, kn_ref,
                       carry_ref, kmax_ref, *, tm, gb):
    @pl.when(pl.program_id(1) == 0)
    def _():
        carry_ref[...] = jnp.zeros_like(carry_ref)
        kmax_ref[...] = jnp.zeros_like(kmax_ref)

    x = x_ref[0]
    u = (x * (1.0 + mod_ref[0, 1:2, :]) + mod_ref[0, 0:1, :]).astype(BF16)
    d = D_MODEL
    qb = (jnp.dot(u, w_ref[:, 0:d], preferred_element_type=F32) * (FOX_HEAD_DIM ** -0.5)).astype(BF16)
    kb = jnp.dot(u, w_ref[:, d:2 * d], preferred_element_type=F32).astype(BF16)
    q_ref[0] = qb
    k_ref[0] = kb

    def head_norms(t):
        t = t.astype(F32)
        return jnp.sqrt(jnp.dot((t * t).astype(BF16), hs_ref[...], preferred_element_type=F32)) * NORM_SLACK

    qnorm = head_norms(qb)
    kmax_ref[...] = jnp.maximum(kmax_ref[...], jnp.max(head_norms(kb), axis=0, keepdims=True))
    kn_ref[0, 0] = kmax_ref[...]
    v_ref[0] = jnp.dot(u, w_ref[:, 2 * d:3 * d], preferred_element_type=F32).astype(BF16)
    f = jnp.dot(u, w_ref[:, 3 * d:3 * d + LANES], preferred_element_type=F32) + bf_ref[...]
    logf = jnp.minimum(f, 0.0) - jnp.log1p(jnp.exp(-jnp.abs(f)))

    for g in range(tm // gb):
        lf = logf[g * gb:(g + 1) * gb]
        rel = jnp.dot(tri_ref[...], jnp.concatenate(_split3(lf), axis=0), preferred_element_type=F32)
        rcat = jnp.concatenate(_split3(rel), axis=1)
        qaug_ref[0, g * gb:(g + 1) * gb, :] = (
            jnp.dot(rcat, eq_ref[...], preferred_element_type=F32) + oq_ref[...]).astype(BF16)
        kaug_ref[0, g * gb:(g + 1) * gb, :] = (
            jnp.dot(rcat, ek_ref[...], preferred_element_type=F32) + ok_ref[...]).astype(BF16)
        rblk_ref[0, 0, g:g + 1, :] = carry_ref[...]
        qn_ref[0, 0, g:g + 1, :] = jnp.max(qnorm[g * gb:(g + 1) * gb], axis=0, keepdims=True)
        carry_ref[...] = carry_ref[...] + rel[gb - 1:gb, :]


def _fox_aug_constants(gb):
    lane = jnp.arange(LANES)
    row = jnp.arange(SPLIT * LANES)
    piece, head = row // LANES, row % LANES
    valid = head < FOX_HEADS
    eq = (valid[:, None] & (lane[None, :] == (AUG_PER_HEAD * head + piece)[:, None])).astype(BF16)
    ek = -(valid[:, None] & (lane[None, :] == (AUG_PER_HEAD * head + SPLIT + piece)[:, None])).astype(BF16)
    used = lane < AUG_PER_HEAD * FOX_HEADS
    oq = (used & (lane % AUG_PER_HEAD >= SPLIT)).astype(F32)[None, :]
    ok = (used & (lane % AUG_PER_HEAD < SPLIT)).astype(F32)[None, :]
    t = jnp.arange(gb)
    tri = (t[:, None] >= t[None, :]).astype(BF16)
    tri3 = jnp.concatenate([tri] * SPLIT, axis=1)
    hs = (jnp.arange(D_MODEL)[:, None] // FOX_HEAD_DIM == lane[None, :]).astype(BF16)
    return tri3, eq, ek, oq, ok, hs


def _fox_inproj(x, mod, w_in, b_f):
    bsz, s, d = x.shape
    tm, gb = min(ROW_TILE, s), ATTN_BLOCK
    n_pad = 3 * d + LANES
    w = jnp.pad(w_in, ((0, 0), (0, n_pad - w_in.shape[1]))).astype(BF16)
    bf = jnp.pad(b_f, (0, LANES - FOX_HEADS)).reshape(1, LANES).astype(F32)
    tri3, eq, ek, oq, ok, hs = _fox_aug_constants(gb)
    const = lambda shape: pl.BlockSpec(shape, lambda b, i: (0,) * len(shape))
    rows = lambda width: pl.BlockSpec((1, tm, width), lambda b, i: (b, i, 0))
    per_block = pl.BlockSpec((1, 1, tm // gb, LANES), lambda b, i: (b, i, 0, 0))
    return pl.pallas_call(
        functools.partial(_fox_inproj_kernel, tm=tm, gb=gb),
        grid=(bsz, s // tm),
        in_specs=[rows(d), pl.BlockSpec((1, 6, d), lambda b, i: (b, 0, 0)), const(w.shape),
                  const(bf.shape), const(tri3.shape), const(eq.shape), const(ek.shape),
                  const(oq.shape), const(ok.shape), const(hs.shape)],
        out_specs=[rows(d), rows(d), rows(d), rows(LANES), rows(LANES), per_block, per_block,
                   pl.BlockSpec((1, 1, 1, LANES), lambda b, i: (b, i, 0, 0))],
        out_shape=[jax.ShapeDtypeStruct((bsz, s, d), BF16)] * 3
        + [jax.ShapeDtypeStruct((bsz, s, LANES), BF16)] * 2
        + [jax.ShapeDtypeStruct((bsz, s // tm, tm // gb, LANES), F32)] * 2
        + [jax.ShapeDtypeStruct((bsz, s // tm, 1, LANES), F32)],
        scratch_shapes=[pltpu.VMEM((1, LANES), F32), pltpu.VMEM((1, LANES), F32)],
        compiler_params=_compiler_params(("arbitrary", "arbitrary")),
        name="fox_inproj",
    )(x, mod, w, bf, tri3, eq, ek, oq, ok, hs)


def _fox_attn_kernel(r_ref, qn_ref, kn_ref, q_ref, qaug_ref, k_ref, v_ref, kaug_ref, o_ref, *, blk, nblk):
    b, hp, i = pl.program_id(0), pl.program_id(1), pl.program_id(2)
    lane = lax.broadcasted_iota(jnp.int32, (1, LANES), 1)
    q = q_ref[0]
    qa = qaug_ref[0]
    zero = jnp.zeros((), BF16)
    nt = (((1,), (1,)), ((), ()))

    qcat, own, rbase, ubound, first = [], [], [], [], []
    for hh in range(2):
        h = 2 * hp + hh
        own.append((lane >= hh * FOX_HEAD_DIM) & (lane < (hh + 1) * FOX_HEAD_DIM))
        aug = (lane >= AUG_PER_HEAD * h) & (lane < AUG_PER_HEAD * (h + 1))
        qcat.append(jnp.concatenate([jnp.where(own[hh], q, zero), jnp.where(aug, qa, zero)], axis=1))
        base = (b * FOX_HEADS + h) * nblk
        rbase.append(base)
        ub = qn_ref[base + i] * kn_ref[b * FOX_HEADS + h]
        ubound.append(ub)
        thresh = -(SKIP_LOG_MARGIN + 2.0 * ub)
        r_i = r_ref[base + i]
        first.append(lax.while_loop(
            lambda j, base=base, r_i=r_i, thresh=thresh: (j > 0) & (r_i - r_ref[base + j] >= thresh),
            lambda j: j - 1, i))
    j_first = jnp.minimum(first[0], first[1])
    fast = 2.0 * jnp.maximum(ubound[0], ubound[1]) <= FAST_MAX_GAP

    def block_shift(hh, j):
        return r_ref[rbase[hh] + i] - r_ref[rbase[hh] + j]

    def causal_mask(z):
        rr = lax.broadcasted_iota(jnp.int32, (blk, blk), 0)
        cc = lax.broadcasted_iota(jnp.int32, (blk, blk), 1)
        return jnp.where(rr >= cc, z, -jnp.inf)

    def key_slice(ref, j):
        return ref[0, pl.ds(pl.multiple_of(j * blk, blk), blk), :]

    def key_operand(j):
        return jnp.concatenate([key_slice(k_ref, j), key_slice(kaug_ref, j)], axis=1)

    @pl.when(fast)
    def _():
        ones_col = [jnp.broadcast_to(jnp.where(lane == hh, 1.0, 0.0), (blk, LANES)).astype(BF16)
                    for hh in range(2)]

        def tile(j, diagonal):
            kcat = key_operand(j)
            vs = key_slice(v_ref, j)
            out = None
            for hh in range(2):
                z = lax.dot_general(qcat[hh], kcat, nt, preferred_element_type=F32)
                z = z + (block_shift(hh, j) - ubound[hh])
                if diagonal:
                    z = causal_mask(z)
                vaug = jnp.concatenate([jnp.where(own[hh], vs, zero), ones_col[hh]], axis=1)
                part = jnp.dot(jnp.exp(z).astype(BF16), vaug, preferred_element_type=F32)
                out = part if out is None else out + part
            return out

        def finish(acc):
            l = jnp.where(own[0], acc[:, LANES:LANES + 1], acc[:, LANES + 1:LANES + 2])
            o_ref[0] = (acc[:, :LANES] / l).astype(o_ref.dtype)

        @pl.when(j_first < i)
        def _():
            acc = lax.fori_loop(j_first, i - 1, lambda j, a: a + tile(j, False),
                                jnp.zeros((blk, 2 * LANES), F32))
            finish(acc + tile(i - 1, False) + tile(i, True))

        @pl.when(j_first >= i)
        def _():
            finish(tile(i, True))

    @pl.when(jnp.logical_not(fast))
    def _():
        def tile(j, carry, diagonal):
            kcat = key_operand(j)
            vs = key_slice(v_ref, j)
            out = []
            for hh in range(2):
                m, l, acc = carry[hh]
                z = lax.dot_general(qcat[hh], kcat, nt, preferred_element_type=F32)
                dij = block_shift(hh, j)
                if diagonal:
                    z = causal_mask(z)
                m_new = jnp.maximum(m, jnp.max(z, axis=1, keepdims=True) + dij)
                p = jnp.exp(z - (m_new - dij))
                alpha = jnp.exp(m - m_new)
                l = alpha * l + jnp.sum(p, axis=1, keepdims=True)
                vm = jnp.where(own[hh], vs, zero)
                acc = alpha * acc + jnp.dot(p.astype(BF16), vm, preferred_element_type=F32)
                out.append((m_new, l, acc))
            return tuple(out)

        init = tuple((jnp.full((blk, 1), -jnp.inf, F32), jnp.zeros((blk, 1), F32),
                      jnp.zeros((blk, LANES), F32)) for _ in range(2))
        carry = lax.fori_loop(j_first, i, lambda j, c: tile(j, c, False), init)
        carry = tile(i, carry, True)
        (_, l0, a0), (_, l1, a1) = carry
        o_ref[0] = (a0 / l0 + a1 / l1).astype(o_ref.dtype)


def _fox_attention(q, k, v, qaug, kaug, rblk, qnblk, kntile):
    bsz, s, d = q.shape
    blk = ATTN_BLOCK
    nblk = s // blk
    flat = lambda a: a.reshape(bsz, nblk, LANES)[:, :, :FOX_HEADS].transpose(0, 2, 1).reshape(-1)
    knflat = kntile[:, -1, 0, :FOX_HEADS].reshape(-1)
    idx = lambda f: (lambda b, hp, i, r, qn, kn: f(b, hp, i))
    grid_spec = pltpu.PrefetchScalarGridSpec(
        num_scalar_prefetch=3,
        grid=(bsz, FOX_HEADS // 2, nblk),
        in_specs=[
            pl.BlockSpec((1, blk, LANES), idx(lambda b, hp, i: (b, i, hp))),
            pl.BlockSpec((1, blk, LANES), idx(lambda b, hp, i: (b, i, 0))),
            pl.BlockSpec((1, s, LANES), idx(lambda b, hp, i: (b, 0, hp))),
            pl.BlockSpec((1, s, LANES), idx(lambda b, hp, i: (b, 0, hp))),
            pl.BlockSpec((1, s, LANES), idx(lambda b, hp, i: (b, 0, 0))),
        ],
        out_specs=pl.BlockSpec((1, blk, LANES), idx(lambda b, hp, i: (b, i, hp))),
    )
    return pl.pallas_call(
        functools.partial(_fox_attn_kernel, blk=blk, nblk=nblk),
        grid_spec=grid_spec,
        out_shape=jax.ShapeDtypeStruct((bsz, s, d), BF16),
        compiler_params=_compiler_params(("arbitrary", "arbitrary", "arbitrary")),
        name="fox_attention",
    )(flat(rblk), flat(qnblk), knflat, q, qaug, k, v, kaug)


def _proj_ln_kernel(a_ref, w_ref, x_ref, mod_ref, g_ref, b_ref, o_ref, *, gate_row):
    y = jnp.dot(a_ref[0], w_ref[...], preferred_element_type=F32)
    r = RESIDUAL_ALPHA * x_ref[0] + (1.0 + mod_ref[0, gate_row:gate_row + 1, :]) * y
    o_ref[0] = _layer_norm(r, g_ref[...], b_ref[...])


def _proj_residual_ln(a, w, x, mod, ln_g, ln_b, gate_row):
    bsz, s, d = x.shape
    kdim = a.shape[-1]
    tm = min(ROW_TILE, s)
    return pl.pallas_call(
        functools.partial(_proj_ln_kernel, gate_row=gate_row),
        grid=(bsz, s // tm),
        in_specs=[
            pl.BlockSpec((1, tm, kdim), lambda b, i: (b, i, 0)),
            pl.BlockSpec((kdim, d), lambda b, i: (0, 0)),
            pl.BlockSpec((1, tm, d), lambda b, i: (b, i, 0)),
            pl.BlockSpec((1, 6, d), lambda b, i: (b, 0, 0)),
            pl.BlockSpec((1, d), lambda b, i: (0, 0)),
            pl.BlockSpec((1, d), lambda b, i: (0, 0)),
        ],
        out_specs=pl.BlockSpec((1, tm, d), lambda b, i: (b, i, 0)),
        out_shape=jax.ShapeDtypeStruct((bsz, s, d), F32),
        compiler_params=_compiler_params(("arbitrary", "arbitrary")),
        name="proj_residual_ln",
    )(a, w.astype(BF16), x, mod, ln_g.reshape(1, d), ln_b.reshape(1, d))


def _mlp_ln_kernel(x_ref, mod_ref, w1_ref, w2_ref, g_ref, b_ref, o_ref, *, ff_chunk):
    x = x_ref[0]
    u = (x * (1.0 + mod_ref[0, 4:5, :]) + mod_ref[0, 3:4, :]).astype(BF16)
    y = None
    for c in range(D_FF // ff_chunk):
        h = jnp.dot(u, w1_ref[:, c * ff_chunk:(c + 1) * ff_chunk], preferred_element_type=F32)
        h = jnp.square(jnp.maximum(h, 0.0)).astype(BF16)
        part = jnp.dot(h, w2_ref[c * ff_chunk:(c + 1) * ff_chunk, :], preferred_element_type=F32)
        y = part if y is None else y + part
    r = RESIDUAL_ALPHA * x + (1.0 + mod_ref[0, 5:6, :]) * y
    o_ref[0] = _layer_norm(r, g_ref[...], b_ref[...])


def _mlp_residual_ln(x, mod, w1, w2, ln_g, ln_b):
    bsz, s, d = x.shape
    tm = min(ROW_TILE, s)
    return pl.pallas_call(
        functools.partial(_mlp_ln_kernel, ff_chunk=1024),
        grid=(bsz, s // tm),
        in_specs=[
            pl.BlockSpec((1, tm, d), lambda b, i: (b, i, 0)),
            pl.BlockSpec((1, 6, d), lambda b, i: (b, 0, 0)),
            pl.BlockSpec((d, D_FF), lambda b, i: (0, 0)),
            pl.BlockSpec((D_FF, d), lambda b, i: (0, 0)),
            pl.BlockSpec((1, d), lambda b, i: (0, 0)),
            pl.BlockSpec((1, d), lambda b, i: (0, 0)),
        ],
        out_specs=pl.BlockSpec((1, tm, d), lambda b, i: (b, i, 0)),
        out_shape=jax.ShapeDtypeStruct((bsz, s, d), F32),
        compiler_params=_compiler_params(("arbitrary", "arbitrary")),
        name="mlp_residual_ln",
    )(x, mod, w1.astype(BF16), w2.astype(BF16), ln_g.reshape(1, d), ln_b.reshape(1, d))


def _ssd_inproj_kernel(x_ref, mod_ref, w_ref, convw_ref, convb_ref, z_ref, xs_ref, bc_ref, dt_ref, ext_ref,
                       *, tm, chunk):
    halo = SUBLANES

    @pl.when(pl.program_id(1) == 0)
    def _():
        ext_ref[0:halo, :] = jnp.zeros((halo, SSM_CONV_DIM), F32)

    u = (x_ref[0] * (1.0 + mod_ref[0, 1:2, :]) + mod_ref[0, 0:1, :]).astype(BF16)
    for c in range(SSM_D_INNER // chunk):
        z_ref[0, :, c * chunk:(c + 1) * chunk] = jnp.dot(
            u, w_ref[:, c * chunk:(c + 1) * chunk], preferred_element_type=F32)
    for c in range(SSM_CONV_DIM // chunk):
        cols = slice(c * chunk, (c + 1) * chunk)
        ext_ref[halo:halo + tm, cols] = jnp.dot(
            u, w_ref[:, SSM_D_INNER + c * chunk:SSM_D_INNER + (c + 1) * chunk], preferred_element_type=F32)
        ext = ext_ref[:, cols]
        ext1 = pltpu.roll(ext, 1, axis=0)
        near = ext * convw_ref[3:4, cols] + ext1 * convw_ref[2:3, cols]
        far = ext * convw_ref[1:2, cols] + ext1 * convw_ref[0:1, cols]
        conv = (near + pltpu.roll(far, 2, axis=0))[halo:halo + tm] + convb_ref[:, cols]
        ext_ref[0:halo, cols] = ext_ref[tm:tm + halo, cols]
        act = _silu(conv)
        if (c + 1) * chunk <= SSM_D_INNER:
            xs_ref[0, :, cols] = act
        else:
            bc_ref[0, :, c * chunk - SSM_D_INNER:(c + 1) * chunk - SSM_D_INNER] = act.astype(BF16)
    off = SSM_D_INNER + SSM_CONV_DIM
    dt_ref[0] = jnp.dot(u, w_ref[:, off:off + LANES], preferred_element_type=F32)


def _ssd_inproj(x, mod, w_in, conv_w, conv_b):
    bsz, s, d = x.shape
    tm = min(ROW_TILE, s)
    n_pad = SSM_D_INNER + SSM_CONV_DIM + LANES
    w = jnp.pad(w_in, ((0, 0), (0, n_pad - w_in.shape[1]))).astype(BF16)
    rows = lambda width: pl.BlockSpec((1, tm, width), lambda b, i: (b, i, 0))
    const = lambda shape: pl.BlockSpec(shape, lambda b, i: (0,) * len(shape))
    return pl.pallas_call(
        functools.partial(_ssd_inproj_kernel, tm=tm, chunk=1024),
        grid=(bsz, s // tm),
        in_specs=[rows(d), pl.BlockSpec((1, 6, d), lambda b, i: (b, 0, 0)),
                  pl.BlockSpec(w.shape, lambda b, i: (0, 0), pipeline_mode=pl.Buffered(1)),
                  const((SSM_CONV, SSM_CONV_DIM)), const((1, SSM_CONV_DIM))],
        out_specs=[rows(SSM_D_INNER), rows(SSM_D_INNER), rows(2 * SSM_BC_WIDTH), rows(LANES)],
        out_shape=[jax.ShapeDtypeStruct((bsz, s, SSM_D_INNER), F32),
                   jax.ShapeDtypeStruct((bsz, s, SSM_D_INNER), F32),
                   jax.ShapeDtypeStruct((bsz, s, 2 * SSM_BC_WIDTH), BF16),
                   jax.ShapeDtypeStruct((bsz, s, LANES), F32)],
        scratch_shapes=[pltpu.VMEM((SUBLANES + tm, SSM_CONV_DIM), F32)],
        compiler_params=_compiler_params(("arbitrary", "arbitrary")),
        name="ssd_inproj",
    )(x, mod, w, conv_w.astype(F32), conv_b.reshape(1, -1).astype(F32))


def _ssd_core_kernel(xs_ref, bc_ref, z_ref, dt_ref, dtb_ref, alog_ref, dskip_ref,
                     normw_ref, tri_ref, ea_ref, eb_ref, oa_ref, ob_ref, xp_ref,
                     o_ref, state_ref, *, nchunks):
    ln = SSM_CHUNK

    @pl.when(pl.program_id(1) == 0)
    def _():
        state_ref[...] = jnp.zeros_like(state_ref)

    for chunk in range(nchunks):
        _ssd_chunk(slice(chunk * ln, (chunk + 1) * ln), xs_ref, bc_ref, z_ref, dt_ref, dtb_ref, alog_ref,
                   dskip_ref, normw_ref, tri_ref, ea_ref, eb_ref, oa_ref, ob_ref, xp_ref, o_ref, state_ref)


def _ssd_chunk(rows, xs_ref, bc_ref, z_ref, dt_ref, dtb_ref, alog_ref, dskip_ref, normw_ref, tri_ref,
               ea_ref, eb_ref, oa_ref, ob_ref, xp_ref, o_ref, state_ref):
    ln = SSM_CHUNK
    xs = xs_ref[0, rows, :]

    dt = _softplus(dt_ref[0, rows, :] + dtb_ref[...])
    dt_rows = dt.T
    dta = dt * (-jnp.exp(alog_ref[...]))
    acs = jnp.dot(tri_ref[...], jnp.concatenate(_split3(dta), axis=0), preferred_element_type=F32)
    acs_last = acs[ln - 1:ln, :]
    eacs = jnp.exp(acs)
    dtdte = dt * jnp.exp(acs_last - acs)

    acat = jnp.concatenate(_split3(acs), axis=1)
    a_aug = (jnp.dot(acat, ea_ref[...], preferred_element_type=F32) + oa_ref[...]).astype(BF16)
    b_aug = (jnp.dot(acat, eb_ref[...], preferred_element_type=F32) + ob_ref[...]).astype(BF16)

    def expand(v):
        return jnp.dot(jnp.concatenate(_split3(v), axis=1), xp_ref[...], preferred_element_type=F32)

    xb = xs.astype(BF16)
    xdte = (xs * expand(dtdte)).astype(BF16)
    eacs_x = expand(eacs)

    rr = lax.broadcasted_iota(jnp.int32, (ln, ln), 0)
    cc = lax.broadcasted_iota(jnp.int32, (ln, ln), 1)
    causal = rr >= cc
    lane_aug = lax.broadcasted_iota(jnp.int32, (1, 2 * LANES), 1)
    lane_grp = lax.broadcasted_iota(jnp.int32, (1, SSM_GROUP_WIDTH), 1)
    zero = jnp.zeros((), BF16)
    nt = (((1,), (1,)), ((), ()))
    tn = (((0,), (0,)), ((), ()))

    for g in range(SSM_GROUPS):
        gs = slice(g * SSM_GROUP_WIDTH, (g + 1) * SSM_GROUP_WIDTH)
        bg = bc_ref[0, rows, g * SSM_STATE:(g + 1) * SSM_STATE]
        cg = bc_ref[0, rows, SSM_BC_WIDTH + g * SSM_STATE:SSM_BC_WIDTH + (g + 1) * SSM_STATE]
        cb = lax.dot_general(cg, bg, nt, preferred_element_type=F32)
        xg = xb[:, gs]
        y = None
        for r in range(SSM_HEADS_PER_GROUP):
            h = g * SSM_HEADS_PER_GROUP + r
            sel = (lane_aug >= AUG_PER_HEAD * h) & (lane_aug < AUG_PER_HEAD * (h + 1))
            seg = lax.dot_general(jnp.where(sel, a_aug, zero), b_aug, nt, preferred_element_type=F32)
            decay = jnp.exp(jnp.where(causal, seg, -jnp.inf))
            own = (lane_grp >= r * SSM_HEAD_DIM) & (lane_grp < (r + 1) * SSM_HEAD_DIM)
            mix = (cb * decay * dt_rows[h:h + 1, :]).astype(BF16)
            part = jnp.dot(mix, jnp.where(own, xg, zero), preferred_element_type=F32)
            y = part if y is None else y + part
        prev = state_ref[g]
        y = y + jnp.dot(cg, prev.astype(BF16), preferred_element_type=F32) * eacs_x[:, gs]
        y = y + xs[:, gs] * dskip_ref[:, gs]
        new_states = lax.dot_general(bg, xdte[:, gs], tn, preferred_element_type=F32)
        state_ref[g] = prev * eacs_x[ln - 1:ln, gs] + new_states
        zg = z_ref[0, rows, gs]
        gated = y * _silu(zg)
        ms = jnp.mean(gated * gated, axis=-1, keepdims=True)
        o_ref[0, rows, gs] = (gated * lax.rsqrt(ms + RMS_EPS) * normw_ref[:, gs]).astype(o_ref.dtype)


def _ssd_constants():
    ln = SSM_CHUNK
    t = jnp.arange(ln)
    tri = (t[:, None] >= t[None, :]).astype(BF16)
    tri3 = jnp.concatenate([tri] * SPLIT, axis=1)
    lane = jnp.arange(2 * LANES)
    row = jnp.arange(SPLIT * LANES)
    piece, head = row // LANES, row % LANES
    valid = head < SSM_HEADS
    ea = (valid[:, None] & (lane[None, :] == (AUG_PER_HEAD * head + piece)[:, None])).astype(BF16)
    eb = -(valid[:, None] & (lane[None, :] == (AUG_PER_HEAD * head + SPLIT + piece)[:, None])).astype(BF16)
    used = lane < AUG_PER_HEAD * SSM_HEADS
    oa = (used & (lane % AUG_PER_HEAD >= SPLIT)).astype(F32)[None, :]
    ob = (used & (lane % AUG_PER_HEAD < SPLIT)).astype(F32)[None, :]
    col = jnp.arange(SSM_D_INNER)
    xp = (valid[:, None] & (head[:, None] == (col // SSM_HEAD_DIM)[None, :])).astype(BF16)
    return tri3, ea, eb, oa, ob, xp


def _ssd_core(z, xs, bc, dt, dt_bias, a_log, d_skip, norm_w):
    bsz, s, _ = z.shape
    nchunks = SSD_CHUNKS_PER_STEP
    ln = nchunks * SSM_CHUNK
    pad_heads = lambda v: jnp.pad(v.astype(F32), (0, LANES - SSM_HEADS)).reshape(1, LANES)
    consts = _ssd_constants()
    const = lambda a: pl.BlockSpec(a.shape, lambda b, i: (0,) * a.ndim)
    rows = lambda width: pl.BlockSpec((1, ln, width), lambda b, i: (b, i, 0))
    small = [pad_heads(dt_bias), pad_heads(a_log),
             jnp.repeat(d_skip.astype(F32), SSM_HEAD_DIM).reshape(1, -1), norm_w.reshape(1, -1).astype(F32)]
    return pl.pallas_call(
        functools.partial(_ssd_core_kernel, nchunks=nchunks),
        grid=(bsz, s // ln),
        in_specs=[rows(SSM_D_INNER), rows(2 * SSM_BC_WIDTH), rows(SSM_D_INNER), rows(LANES)]
        + [const(a) for a in small] + [const(a) for a in consts],
        out_specs=rows(SSM_D_INNER),
        out_shape=jax.ShapeDtypeStruct((bsz, s, SSM_D_INNER), BF16),
        scratch_shapes=[pltpu.VMEM((SSM_GROUPS, SSM_STATE, SSM_GROUP_WIDTH), F32)],
        compiler_params=_compiler_params(("arbitrary", "arbitrary")),
        name="ssd_core",
    )(xs, bc, z, dt, *small, *consts)


def kernel(x, c, ada_w, ada_b, ln_mix_g, ln_mix_b, ln_mlp_g, ln_mlp_b, mlp_w1, mlp_w2, fox_w_in, fox_b_f, fox_w_o, ssm_w_in, ssm_conv_w, ssm_conv_b, ssm_dt_bias, ssm_a_log, ssm_d, ssm_norm_w, ssm_w_out):
    mod = _ada_modulation(c, ada_w, ada_b)

    q, k, v, qaug, kaug, rblk, qnblk, kntile = _fox_inproj(x, mod[0], fox_w_in[0], fox_b_f[0])
    attn = _fox_attention(q, k, v, qaug, kaug, rblk, qnblk, kntile)
    x = _proj_residual_ln(attn, fox_w_o[0], x, mod[0], ln_mix_g[0], ln_mix_b[0], gate_row=2)
    x = _mlp_residual_ln(x, mod[0], mlp_w1[0], mlp_w2[0], ln_mlp_g[0], ln_mlp_b[0])

    z, xs, bc, dt = _ssd_inproj(x, mod[1], ssm_w_in[0], ssm_conv_w[0], ssm_conv_b[0])
    y = _ssd_core(z, xs, bc, dt, ssm_dt_bias[0], ssm_a_log[0], ssm_d[0], ssm_norm_w[0])
    x = _proj_residual_ln(y, ssm_w_out[0], x, mod[1], ln_mix_g[1], ln_mix_b[1], gate_row=2)
    x = _mlp_residual_ln(x, mod[1], mlp_w1[1], mlp_w2[1], ln_mlp_g[1], ln_mlp_b[1])
    return x
```

```python
import functools

import jax
import jax.numpy as jnp
from jax import lax
from jax.experimental import pallas as pl
from jax.experimental.pallas import tpu as pltpu

F32 = jnp.float32
BF16 = jnp.bfloat16

D_MODEL = 1024
DEPTH = 2
FOX_HEADS = 16
FOX_HEAD_DIM = D_MODEL // FOX_HEADS
SSM_D_INNER = 2 * D_MODEL
SSM_HEAD_DIM = 64
SSM_HEADS = SSM_D_INNER // SSM_HEAD_DIM
SSM_GROUPS = 8
SSM_HEADS_PER_GROUP = SSM_HEADS // SSM_GROUPS
SSM_STATE = 128
SSM_CONV = 4
SSM_CHUNK = 128
SSM_GROUP_WIDTH = SSM_D_INNER // SSM_GROUPS
SSM_BC_WIDTH = SSM_GROUPS * SSM_STATE
SSM_CONV_DIM = SSM_D_INNER + 2 * SSM_BC_WIDTH
D_FF = 4 * D_MODEL
LN_EPS = 1e-5
RMS_EPS = 1e-5
RESIDUAL_ALPHA = (2.0 * DEPTH) ** 0.25

LANES = 128
SUBLANES = 8
VMEM_LIMIT_BYTES = 56 * 1024 * 1024

ATTN_BLOCK = 512
ATTN_PAIRS_PER_STEP = 2
ROW_TILE = 512
SSD_CHUNKS_PER_STEP = 2
AUG_PER_HEAD = 6
SPLIT = 3
NORM_SLACK = 1.01
SKIP_LOG_MARGIN = 32.0
FAST_MAX_GAP = 60.0


def _split3(v):
    hi = v.astype(BF16)
    r1 = v - hi.astype(F32)
    mid = r1.astype(BF16)
    lo = (r1 - mid.astype(F32)).astype(BF16)
    return hi, mid, lo


def _softplus(y):
    return jnp.maximum(y, 0.0) + jnp.log1p(jnp.exp(-jnp.abs(y)))


def _silu(y):
    h = 0.5 * y
    return h + h * jnp.tanh(h)


def _layer_norm(r, g, b):
    mu = jnp.mean(r, axis=-1, keepdims=True)
    cen = r - mu
    var = jnp.mean(cen * cen, axis=-1, keepdims=True)
    return cen * lax.rsqrt(var + LN_EPS) * g + b


def _compiler_params(semantics):
    return pltpu.CompilerParams(dimension_semantics=semantics, vmem_limit_bytes=VMEM_LIMIT_BYTES)


def _ada_kernel(c_ref, w_ref, b_ref, o_ref):
    c = c_ref[...]
    cond = _silu(c)
    o_ref[0] = jnp.dot(cond, w_ref[0], precision=lax.Precision.HIGHEST,
                       preferred_element_type=F32) + b_ref[0]


def _ada_modulation(c, ada_w, ada_b):
    depth, d, n = ada_w.shape
    bsz = c.shape[0]
    tn = 1536
    out = pl.pallas_call(
        _ada_kernel,
        grid=(depth, n // tn),
        in_specs=[
            pl.BlockSpec((bsz, d), lambda i, j: (0, 0)),
            pl.BlockSpec((1, d, tn), lambda i, j: (i, 0, j)),
            pl.BlockSpec((1, 1, tn), lambda i, j: (i, 0, j)),
        ],
        out_specs=pl.BlockSpec((1, bsz, tn), lambda i, j: (i, 0, j)),
        out_shape=jax.ShapeDtypeStruct((depth, bsz, n), F32),
        compiler_params=_compiler_params(("arbitrary", "arbitrary")),
        name="ada_modulation",
    )(c, ada_w, ada_b.reshape(depth, 1, n))
    return out.reshape(depth, bsz, 6, d)


def _fox_inproj_kernel(x_ref, mod_ref, w_ref, bf_ref, tri_ref, eq_ref, ek_ref, oq_ref, ok_ref, hs_ref,
                       q_ref, k_ref, v_ref, qaug_ref, kaug_ref, rblk_ref, qn_ref, kn_ref,
                       carry_ref, kmax_ref, *, tm, gb):
    @pl.when(pl.program_id(1) == 0)
    def _():
        carry_ref[...] = jnp.zeros_like(carry_ref)
        kmax_ref[...] = jnp.zeros_like(kmax_ref)

    x = x_ref[0]
    u = (x * (1.0 + mod_ref[0, 1:2, :]) + mod_ref[0, 0:1, :]).astype(BF16)
    d = D_MODEL
    qb = (jnp.dot(u, w_ref[:, 0:d], preferred_element_type=F32) * (FOX_HEAD_DIM ** -0.5)).astype(BF16)
    kb = jnp.dot(u, w_ref[:, d:2 * d], preferred_element_type=F32).astype(BF16)
    q_ref[0] = qb
    k_ref[0] = kb

    def head_norms(t):
        t = t.astype(F32)
        return jnp.sqrt(jnp.dot((t * t).astype(BF16), hs_ref[...], preferred_element_type=F32)) * NORM_SLACK

    qnorm = head_norms(qb)
    kmax_ref[...] = jnp.maximum(kmax_ref[...], jnp.max(head_norms(kb), axis=0, keepdims=True))
    kn_ref[0, 0] = kmax_ref[...]
    v_ref[0] = jnp.dot(u, w_ref[:, 2 * d:3 * d], preferred_element_type=F32).astype(BF16)
    f = jnp.dot(u, w_ref[:, 3 * d:3 * d + LANES], preferred_element_type=F32) + bf_ref[...]
    logf = jnp.minimum(f, 0.0) - jnp.log1p(jnp.exp(-jnp.abs(f)))

    for g in range(tm // gb):
        lf = logf[g * gb:(g + 1) * gb]
        rel = jnp.dot(tri_ref[...], jnp.concatenate(_split3(lf), axis=0), preferred_element_type=F32)
        rcat = jnp.concatenate(_split3(rel), axis=1)
        qaug_ref[0, g * gb:(g + 1) * gb, :] = (
            jnp.dot(rcat, eq_ref[...], preferred_element_type=F32) + oq_ref[...]).astype(BF16)
        kaug_ref[0, g * gb:(g + 1) * gb, :] = (
            jnp.dot(rcat, ek_ref[...], preferred_element_type=F32) + ok_ref[...]).astype(BF16)
        rblk_ref[0, 0, g:g + 1, :] = carry_ref[...]
        qn_ref[0, 0, g:g + 1, :] = jnp.max(qnorm[g * gb:(g + 1) * gb], axis=0, keepdims=True)
        carry_ref[...] = carry_ref[...] + rel[gb - 1:gb, :]


def _fox_aug_constants(gb):
    lane = jnp.arange(LANES)
    row = jnp.arange(SPLIT * LANES)
    piece, head = row // LANES, row % LANES
    valid = head < FOX_HEADS
    eq = (valid[:, None] & (lane[None, :] == (AUG_PER_HEAD * head + piece)[:, None])).astype(BF16)
    ek = -(valid[:, None] & (lane[None, :] == (AUG_PER_HEAD * head + SPLIT + piece)[:, None])).astype(BF16)
    used = lane < AUG_PER_HEAD * FOX_HEADS
    oq = (used & (lane % AUG_PER_HEAD >= SPLIT)).astype(F32)[None, :]
    ok = (used & (lane % AUG_PER_HEAD < SPLIT)).astype(F32)[None, :]
    t = jnp.arange(gb)
    tri = (t[:, None] >= t[None, :]).astype(BF16)
    tri3 = jnp.concatenate([tri] * SPLIT, axis=1)
    hs = (jnp.arange(D_MODEL)[:, None] // FOX_HEAD_DIM == lane[None, :]).astype(BF16)
    return tri3, eq, ek, oq, ok, hs


def _fox_inproj(x, mod, w_in, b_f):
    bsz, s, d = x.shape
    tm, gb = min(ROW_TILE, s), ATTN_BLOCK
    n_pad = 3 * d + LANES
    w = jnp.pad(w_in, ((0, 0), (0, n_pad - w_in.shape[1]))).astype(BF16)
    bf = jnp.pad(b_f, (0, LANES - FOX_HEADS)).reshape(1, LANES).astype(F32)
    tri3, eq, ek, oq, ok, hs = _fox_aug_constants(gb)
    const = lambda shape: pl.BlockSpec(shape, lambda b, i: (0,) * len(shape))
    rows = lambda width: pl.BlockSpec((1, tm, width), lambda b, i: (b, i, 0))
    per_block = pl.BlockSpec((1, 1, tm // gb, LANES), lambda b, i: (b, i, 0, 0))
    return pl.pallas_call(
        functools.partial(_fox_inproj_kernel, tm=tm, gb=gb),
        grid=(bsz, s // tm),
        in_specs=[rows(d), pl.BlockSpec((1, 6, d), lambda b, i: (b, 0, 0)), const(w.shape),
                  const(bf.shape), const(tri3.shape), const(eq.shape), const(ek.shape),
                  const(oq.shape), const(ok.shape), const(hs.shape)],
        out_specs=[rows(d), rows(d), rows(d), rows(LANES), rows(LANES), per_block, per_block,
                   pl.BlockSpec((1, 1, 1, LANES), lambda b, i: (b, i, 0, 0))],
        out_shape=[jax.ShapeDtypeStruct((bsz, s, d), BF16)] * 3
        + [jax.ShapeDtypeStruct((bsz, s, LANES), BF16)] * 2
        + [jax.ShapeDtypeStruct((bsz, s // tm, tm // gb, LANES), F32)] * 2
        + [jax.ShapeDtypeStruct((bsz, s // tm, 1, LANES), F32)],
        scratch_shapes=[pltpu.VMEM((1, LANES), F32), pltpu.VMEM((1, LANES), F32)],
        compiler_params=_compiler_params(("arbitrary", "arbitrary")),
        name="fox_inproj",
    )(x, mod, w, bf, tri3, eq, ek, oq, ok, hs)


def _fox_attn_kernel(r_ref, qn_ref, kn_ref, q_ref, qaug_ref, k_ref, v_ref, kaug_ref, o_ref, *, blk, nblk, npairs):
    b, hg, i = pl.program_id(0), pl.program_id(1), pl.program_id(2)
    nheads = 2 * npairs
    lane = lax.broadcasted_iota(jnp.int32, (1, LANES), 1)
    qa = qaug_ref[0]
    zero = jnp.zeros((), BF16)
    nt = (((1,), (1,)), ((), ()))
    slab = lambda pair: slice(pair * LANES, (pair + 1) * LANES)

    qcat, own, rbase, ubound, first = [], [], [], [], []
    for hh in range(nheads):
        h = nheads * hg + hh
        half = hh % 2
        own.append((lane >= half * FOX_HEAD_DIM) & (lane < (half + 1) * FOX_HEAD_DIM))
        aug = (lane >= AUG_PER_HEAD * h) & (lane < AUG_PER_HEAD * (h + 1))
        qcat.append(jnp.concatenate([jnp.where(own[hh], q_ref[0, :, slab(hh // 2)], zero),
                                     jnp.where(aug, qa, zero)], axis=1))
        base = (b * FOX_HEADS + h) * nblk
        rbase.append(base)
        ub = qn_ref[base + i] * kn_ref[b * FOX_HEADS + h]
        ubound.append(ub)
        thresh = -(SKIP_LOG_MARGIN + 2.0 * ub)
        r_i = r_ref[base + i]
        first.append(lax.while_loop(
            lambda j, base=base, r_i=r_i, thresh=thresh: (j > 0) & (r_i - r_ref[base + j] >= thresh),
            lambda j: j - 1, i))
    j_first = functools.reduce(jnp.minimum, first)
    fast = 2.0 * functools.reduce(jnp.maximum, ubound) <= FAST_MAX_GAP

    def block_shift(hh, j):
        return r_ref[rbase[hh] + i] - r_ref[rbase[hh] + j]

    def causal_mask(z):
        rr = lax.broadcasted_iota(jnp.int32, (blk, blk), 0)
        cc = lax.broadcasted_iota(jnp.int32, (blk, blk), 1)
        return jnp.where(rr >= cc, z, -jnp.inf)

    def key_rows(j):
        return pl.ds(pl.multiple_of(j * blk, blk), blk)

    def key_operand(j, pair):
        return jnp.concatenate([k_ref[0, key_rows(j), slab(pair)], kaug_ref[0, key_rows(j), :]], axis=1)

    @pl.when(fast)
    def _():
        ones_col = [jnp.broadcast_to(jnp.where(lane == half, 1.0, 0.0), (blk, LANES)).astype(BF16)
                    for half in range(2)]

        def tile(j, diagonal):
            outs = []
            for pair in range(npairs):
                kcat = key_operand(j, pair)
                vs = v_ref[0, key_rows(j), slab(pair)]
                out = None
                for half in range(2):
                    hh = 2 * pair + half
                    z = lax.dot_general(qcat[hh], kcat, nt, preferred_element_type=F32)
                    z = z + (block_shift(hh, j) - ubound[hh])
                    if diagonal:
                        z = causal_mask(z)
                    vaug = jnp.concatenate([jnp.where(own[hh], vs, zero), ones_col[half]], axis=1)
                    part = jnp.dot(jnp.exp(z).astype(BF16), vaug, preferred_element_type=F32)
                    out = part if out is None else out + part
                outs.append(out)
            return tuple(outs)

        def add(accs, parts):
            return tuple(a + p for a, p in zip(accs, parts))

        def finish(accs):
            for pair, acc in enumerate(accs):
                l = jnp.where(own[0], acc[:, LANES:LANES + 1], acc[:, LANES + 1:LANES + 2])
                o_ref[0, :, slab(pair)] = (acc[:, :LANES] / l).astype(o_ref.dtype)

        @pl.when(j_first < i)
        def _():
            init = tuple(jnp.zeros((blk, 2 * LANES), F32) for _ in range(npairs))
            accs = lax.fori_loop(j_first, i - 1, lambda j, a: add(a, tile(j, False)), init)
            finish(add(add(accs, tile(i - 1, False)), tile(i, True)))

        @pl.when(j_first >= i)
        def _():
            finish(tile(i, True))

    @pl.when(jnp.logical_not(fast))
    def _():
        def tile(j, carry, diagonal):
            out = []
            for hh in range(nheads):
                m, l, acc = carry[hh]
                z = lax.dot_general(qcat[hh], key_operand(j, hh // 2), nt, preferred_element_type=F32)
                dij = block_shift(hh, j)
                if diagonal:
                    z = causal_mask(z)
                m_new = jnp.maximum(m, jnp.max(z, axis=1, keepdims=True) + dij)
                p = jnp.exp(z - (m_new - dij))
                alpha = jnp.exp(m - m_new)
                l = alpha * l + jnp.sum(p, axis=1, keepdims=True)
                vm = jnp.where(own[hh], v_ref[0, key_rows(j), slab(hh // 2)], zero)
                acc = alpha * acc + jnp.dot(p.astype(BF16), vm, preferred_element_type=F32)
                out.append((m_new, l, acc))
            return tuple(out)

        init = tuple((jnp.full((blk, 1), -jnp.inf, F32), jnp.zeros((blk, 1), F32),
                      jnp.zeros((blk, LANES), F32)) for _ in range(nheads))
        carry = lax.fori_loop(j_first, i, lambda j, c: tile(j, c, False), init)
        carry = tile(i, carry, True)
        for pair in range(npairs):
            (_, l0, a0), (_, l1, a1) = carry[2 * pair], carry[2 * pair + 1]
            o_ref[0, :, slab(pair)] = (a0 / l0 + a1 / l1).astype(o_ref.dtype)


def _fox_attention(q, k, v, qaug, kaug, rblk, qnblk, kntile):
    bsz, s, d = q.shape
    blk = ATTN_BLOCK
    nblk = s // blk
    npairs = ATTN_PAIRS_PER_STEP
    width = npairs * LANES
    flat = lambda a: a.reshape(bsz, nblk, LANES)[:, :, :FOX_HEADS].transpose(0, 2, 1).reshape(-1)
    knflat = kntile[:, -1, 0, :FOX_HEADS].reshape(-1)
    idx = lambda f: (lambda b, hg, i, r, qn, kn: f(b, hg, i))
    grid_spec = pltpu.PrefetchScalarGridSpec(
        num_scalar_prefetch=3,
        grid=(bsz, FOX_HEADS // (2 * npairs), nblk),
        in_specs=[
            pl.BlockSpec((1, blk, width), idx(lambda b, hg, i: (b, i, hg))),
            pl.BlockSpec((1, blk, LANES), idx(lambda b, hg, i: (b, i, 0))),
            pl.BlockSpec((1, s, width), idx(lambda b, hg, i: (b, 0, hg))),
            pl.BlockSpec((1, s, width), idx(lambda b, hg, i: (b, 0, hg))),
            pl.BlockSpec((1, s, LANES), idx(lambda b, hg, i: (b, 0, 0))),
        ],
        out_specs=pl.BlockSpec((1, blk, width), idx(lambda b, hg, i: (b, i, hg))),
    )
    return pl.pallas_call(
        functools.partial(_fox_attn_kernel, blk=blk, nblk=nblk, npairs=npairs),
        grid_spec=grid_spec,
        out_shape=jax.ShapeDtypeStruct((bsz, s, d), BF16),
        compiler_params=_compiler_params(("arbitrary", "arbitrary", "arbitrary")),
        name="fox_attention",
    )(flat(rblk), flat(qnblk), knflat, q, qaug, k, v, kaug)


def _proj_mlp_kernel(a_ref, wo_ref, x_ref, mod_ref, g1_ref, b1_ref, w1_ref, w2_ref, g2_ref, b2_ref, o_ref,
                     *, ff_chunk):
    y = jnp.dot(a_ref[0], wo_ref[...], preferred_element_type=F32)
    x1 = _layer_norm(RESIDUAL_ALPHA * x_ref[0] + (1.0 + mod_ref[0, 2:3, :]) * y, g1_ref[...], b1_ref[...])
    u = (x1 * (1.0 + mod_ref[0, 4:5, :]) + mod_ref[0, 3:4, :]).astype(BF16)
    y = None
    for c in range(D_FF // ff_chunk):
        h = jnp.dot(u, w1_ref[:, c * ff_chunk:(c + 1) * ff_chunk], preferred_element_type=F32)
        h = jnp.square(jnp.maximum(h, 0.0)).astype(BF16)
        part = jnp.dot(h, w2_ref[c * ff_chunk:(c + 1) * ff_chunk, :], preferred_element_type=F32)
        y = part if y is None else y + part
    r = RESIDUAL_ALPHA * x1 + (1.0 + mod_ref[0, 5:6, :]) * y
    o_ref[0] = _layer_norm(r, g2_ref[...], b2_ref[...])


def _proj_mlp(a, w_o, x, mod, ln1_g, ln1_b, w1, w2, ln2_g, ln2_b):
    bsz, s, d = x.shape
    kdim = a.shape[-1]
    tm = min(ROW_TILE, s)
    rows = lambda width: pl.BlockSpec((1, tm, width), lambda b, i: (b, i, 0))
    resident = lambda shape: pl.BlockSpec(shape, lambda b, i: (0,) * len(shape), pipeline_mode=pl.Buffered(1))
    vec = pl.BlockSpec((1, d), lambda b, i: (0, 0))
    return pl.pallas_call(
        functools.partial(_proj_mlp_kernel, ff_chunk=1024),
        grid=(bsz, s // tm),
        in_specs=[rows(kdim), resident((kdim, d)), rows(d), pl.BlockSpec((1, 6, d), lambda b, i: (b, 0, 0)),
                  vec, vec, resident((d, D_FF)), resident((D_FF, d)), vec, vec],
        out_specs=rows(d),
        out_shape=jax.ShapeDtypeStruct((bsz, s, d), F32),
        compiler_params=_compiler_params(("arbitrary", "arbitrary")),
        name="proj_mlp",
    )(a, w_o.astype(BF16), x, mod, ln1_g.reshape(1, d), ln1_b.reshape(1, d), w1.astype(BF16), w2.astype(BF16),
      ln2_g.reshape(1, d), ln2_b.reshape(1, d))


def _ssd_inproj_kernel(x_ref, mod_ref, w_ref, convw_ref, convb_ref, z_ref, xs_ref, bc_ref, dt_ref, ext_ref,
                       *, tm, chunk):
    halo = SUBLANES

    @pl.when(pl.program_id(1) == 0)
    def _():
        ext_ref[0:halo, :] = jnp.zeros((halo, SSM_CONV_DIM), F32)

    u = (x_ref[0] * (1.0 + mod_ref[0, 1:2, :]) + mod_ref[0, 0:1, :]).astype(BF16)
    for c in range(SSM_D_INNER // chunk):
        z_ref[0, :, c * chunk:(c + 1) * chunk] = jnp.dot(
            u, w_ref[:, c * chunk:(c + 1) * chunk], preferred_element_type=F32)
    for c in range(SSM_CONV_DIM // chunk):
        cols = slice(c * chunk, (c + 1) * chunk)
        ext_ref[halo:halo + tm, cols] = jnp.dot(
            u, w_ref[:, SSM_D_INNER + c * chunk:SSM_D_INNER + (c + 1) * chunk], preferred_element_type=F32)
        ext = ext_ref[:, cols]
        ext1 = pltpu.roll(ext, 1, axis=0)
        near = ext * convw_ref[3:4, cols] + ext1 * convw_ref[2:3, cols]
        far = ext * convw_ref[1:2, cols] + ext1 * convw_ref[0:1, cols]
        conv = (near + pltpu.roll(far, 2, axis=0))[halo:halo + tm] + convb_ref[:, cols]
        ext_ref[0:halo, cols] = ext_ref[tm:tm + halo, cols]
        act = _silu(conv)
        if (c + 1) * chunk <= SSM_D_INNER:
            xs_ref[0, :, cols] = act
        else:
            bc_ref[0, :, c * chunk - SSM_D_INNER:(c + 1) * chunk - SSM_D_INNER] = act.astype(BF16)
    off = SSM_D_INNER + SSM_CONV_DIM
    dt_ref[0] = jnp.dot(u, w_ref[:, off:off + LANES], preferred_element_type=F32)


def _ssd_inproj(x, mod, w_in, conv_w, conv_b):
    bsz, s, d = x.shape
    tm = min(ROW_TILE, s)
    n_pad = SSM_D_INNER + SSM_CONV_DIM + LANES
    w = jnp.pad(w_in, ((0, 0), (0, n_pad - w_in.shape[1]))).astype(BF16)
    rows = lambda width: pl.BlockSpec((1, tm, width), lambda b, i: (b, i, 0))
    const = lambda shape: pl.BlockSpec(shape, lambda b, i: (0,) * len(shape))
    return pl.pallas_call(
        functools.partial(_ssd_inproj_kernel, tm=tm, chunk=1024),
        grid=(bsz, s // tm),
        in_specs=[rows(d), pl.BlockSpec((1, 6, d), lambda b, i: (b, 0, 0)),
                  pl.BlockSpec(w.shape, lambda b, i: (0, 0), pipeline_mode=pl.Buffered(1)),
                  const((SSM_CONV, SSM_CONV_DIM)), const((1, SSM_CONV_DIM))],
        out_specs=[rows(SSM_D_INNER), rows(SSM_D_INNER), rows(2 * SSM_BC_WIDTH), rows(LANES)],
        out_shape=[jax.ShapeDtypeStruct((bsz, s, SSM_D_INNER), F32),
                   jax.ShapeDtypeStruct((bsz, s, SSM_D_INNER), F32),
                   jax.ShapeDtypeStruct((bsz, s, 2 * SSM_BC_WIDTH), BF16),
                   jax.ShapeDtypeStruct((bsz, s, LANES), F32)],
        scratch_shapes=[pltpu.VMEM((SUBLANES + tm, SSM_CONV_DIM), F32)],
        compiler_params=_compiler_params(("arbitrary", "arbitrary")),
        name="ssd_inproj",
    )(x, mod, w, conv_w.astype(F32), conv_b.reshape(1, -1).astype(F32))


def _ssd_core_kernel(xs_ref, bc_ref, z_ref, dt_ref, dtb_ref, alog_ref, dskip_ref,
                     normw_ref, tri_ref, xp_ref,
                     o_ref, state_ref, *, nchunks):
    ln = SSM_CHUNK

    @pl.when(pl.program_id(1) == 0)
    def _():
        state_ref[...] = jnp.zeros_like(state_ref)

    for chunk in range(nchunks):
        _ssd_chunk(slice(chunk * ln, (chunk + 1) * ln), xs_ref, bc_ref, z_ref, dt_ref, dtb_ref, alog_ref,
                   dskip_ref, normw_ref, tri_ref, xp_ref, o_ref, state_ref)


def _ssd_chunk(rows, xs_ref, bc_ref, z_ref, dt_ref, dtb_ref, alog_ref, dskip_ref, normw_ref, tri_ref,
               xp_ref, o_ref, state_ref):
    ln = SSM_CHUNK
    xs = xs_ref[0, rows, :]

    dt = _softplus(dt_ref[0, rows, :] + dtb_ref[...])
    dt_rows = dt.T
    dta = dt * (-jnp.exp(alog_ref[...]))
    acs = jnp.dot(tri_ref[...], jnp.concatenate(_split3(dta), axis=0), preferred_element_type=F32)
    acs_last = acs[ln - 1:ln, :]
    dtdte = dt * jnp.exp(acs_last - acs)
    acs_rows = acs.T

    def expand(v):
        return jnp.dot(jnp.concatenate(_split3(v), axis=1), xp_ref[...], preferred_element_type=F32)

    xb = xs.astype(BF16)
    xdte = (xs * expand(dtdte)).astype(BF16)
    eacs_x = expand(jnp.exp(acs))

    rr = lax.broadcasted_iota(jnp.int32, (ln, ln), 0)
    cc = lax.broadcasted_iota(jnp.int32, (ln, ln), 1)
    causal = rr >= cc
    lane_grp = lax.broadcasted_iota(jnp.int32, (1, SSM_GROUP_WIDTH), 1)
    zero = jnp.zeros((), BF16)
    nt = (((1,), (1,)), ((), ()))
    tn = (((0,), (0,)), ((), ()))

    for g in range(SSM_GROUPS):
        gs = slice(g * SSM_GROUP_WIDTH, (g + 1) * SSM_GROUP_WIDTH)
        bg = bc_ref[0, rows, g * SSM_STATE:(g + 1) * SSM_STATE]
        cg = bc_ref[0, rows, SSM_BC_WIDTH + g * SSM_STATE:SSM_BC_WIDTH + (g + 1) * SSM_STATE]
        cb = lax.dot_general(cg, bg, nt, preferred_element_type=F32)
        xg = xb[:, gs]
        y = None
        for r in range(SSM_HEADS_PER_GROUP):
            h = g * SSM_HEADS_PER_GROUP + r
            seg = acs[:, h:h + 1] - acs_rows[h:h + 1, :]
            decay = jnp.exp(jnp.where(causal, seg, -jnp.inf))
            mix = (cb * decay * dt_rows[h:h + 1, :]).astype(BF16)
            own = (lane_grp >= r * SSM_HEAD_DIM) & (lane_grp < (r + 1) * SSM_HEAD_DIM)
            part = jnp.dot(mix, jnp.where(own, xg, zero), preferred_element_type=F32)
            y = part if y is None else y + part
        prev = state_ref[g]
        y = y + jnp.dot(cg, prev.astype(BF16), preferred_element_type=F32) * eacs_x[:, gs]
        y = y + xs[:, gs] * dskip_ref[:, gs]
        new_states = lax.dot_general(bg, xdte[:, gs], tn, preferred_element_type=F32)
        state_ref[g] = prev * eacs_x[ln - 1:ln, gs] + new_states
        zg = z_ref[0, rows, gs]
        gated = y * _silu(zg)
        ms = jnp.mean(gated * gated, axis=-1, keepdims=True)
        o_ref[0, rows, gs] = (gated * lax.rsqrt(ms + RMS_EPS) * normw_ref[:, gs]).astype(o_ref.dtype)


def _ssd_constants():
    ln = SSM_CHUNK
    t = jnp.arange(ln)
    tri = (t[:, None] >= t[None, :]).astype(BF16)
    tri3 = jnp.concatenate([tri] * SPLIT, axis=1)
    head = jnp.arange(SPLIT * LANES) % LANES
    col = jnp.arange(SSM_D_INNER)
    xp = ((head < SSM_HEADS)[:, None] & (head[:, None] == (col // SSM_HEAD_DIM)[None, :])).astype(BF16)
    return tri3, xp


def _ssd_core(z, xs, bc, dt, dt_bias, a_log, d_skip, norm_w):
    bsz, s, _ = z.shape
    nchunks = SSD_CHUNKS_PER_STEP
    ln = nchunks * SSM_CHUNK
    pad_heads = lambda v: jnp.pad(v.astype(F32), (0, LANES - SSM_HEADS)).reshape(1, LANES)
    consts = _ssd_constants()
    const = lambda a: pl.BlockSpec(a.shape, lambda b, i: (0,) * a.ndim)
    rows = lambda width: pl.BlockSpec((1, ln, width), lambda b, i: (b, i, 0))
    small = [pad_heads(dt_bias), pad_heads(a_log),
             jnp.repeat(d_skip.astype(F32), SSM_HEAD_DIM).reshape(1, -1), norm_w.reshape(1, -1).astype(F32)]
    return pl.pallas_call(
        functools.partial(_ssd_core_kernel, nchunks=nchunks),
        grid=(bsz, s // ln),
        in_specs=[rows(SSM_D_INNER), rows(2 * SSM_BC_WIDTH), rows(SSM_D_INNER), rows(LANES)]
        + [const(a) for a in small] + [const(a) for a in consts],
        out_specs=rows(SSM_D_INNER),
        out_shape=jax.ShapeDtypeStruct((bsz, s, SSM_D_INNER), BF16),
        scratch_shapes=[pltpu.VMEM((SSM_GROUPS, SSM_STATE, SSM_GROUP_WIDTH), F32)],
        compiler_params=_compiler_params(("arbitrary", "arbitrary")),
        name="ssd_core",
    )(xs, bc, z, dt, *small, *consts)


def kernel(x, c, ada_w, ada_b, ln_mix_g, ln_mix_b, ln_mlp_g, ln_mlp_b, mlp_w1, mlp_w2, fox_w_in, fox_b_f, fox_w_o, ssm_w_in, ssm_conv_w, ssm_conv_b, ssm_dt_bias, ssm_a_log, ssm_d, ssm_norm_w, ssm_w_out):
    mod = _ada_modulation(c, ada_w, ada_b)

    q, k, v, qaug, kaug, rblk, qnblk, kntile = _fox_inproj(x, mod[0], fox_w_in[0], fox_b_f[0])
    attn = _fox_attention(q, k, v, qaug, kaug, rblk, qnblk, kntile)
    x = _proj_mlp(attn, fox_w_o[0], x, mod[0], ln_mix_g[0], ln_mix_b[0], mlp_w1[0], mlp_w2[0],
                  ln_mlp_g[0], ln_mlp_b[0])

    z, xs, bc, dt = _ssd_inproj(x, mod[1], ssm_w_in[0], ssm_conv_w[0], ssm_conv_b[0])
    y = _ssd_core(z, xs, bc, dt, ssm_dt_bias[0], ssm_a_log[0], ssm_d[0], ssm_norm_w[0])
    x = _proj_mlp(y, ssm_w_out[0], x, mod[1], ln_mix_g[1], ln_mix_b[1], mlp_w1[1], mlp_w2[1],
                  ln_mlp_g[1], ln_mlp_b[1])
    return x
```

```python
import functools

import jax
import jax.numpy as jnp
from jax import lax
from jax.experimental import pallas as pl
from jax.experimental.pallas import tpu as pltpu

F32 = jnp.float32
BF16 = jnp.bfloat16

D_MODEL = 1024
DEPTH = 2
FOX_HEADS = 16
FOX_HEAD_DIM = D_MODEL // FOX_HEADS
SSM_D_INNER = 2 * D_MODEL
SSM_HEAD_DIM = 64
SSM_HEADS = SSM_D_INNER // SSM_HEAD_DIM
SSM_GROUPS = 8
SSM_HEADS_PER_GROUP = SSM_HEADS // SSM_GROUPS
SSM_STATE = 128
SSM_CONV = 4
SSM_CHUNK = 128
SSM_GROUP_WIDTH = SSM_D_INNER // SSM_GROUPS
SSM_BC_WIDTH = SSM_GROUPS * SSM_STATE
SSM_CONV_DIM = SSM_D_INNER + 2 * SSM_BC_WIDTH
D_FF = 4 * D_MODEL
LN_EPS = 1e-5
RMS_EPS = 1e-5
RESIDUAL_ALPHA = (2.0 * DEPTH) ** 0.25

LANES = 128
SUBLANES = 8
VMEM_LIMIT_BYTES = 56 * 1024 * 1024

ATTN_BLOCK = 512
ATTN_PAIRS_PER_STEP = 2
PREFIX_ROWS = 128
ROW_TILE = 512
SSD_CHUNKS_PER_STEP = 2
AUG_PER_HEAD = 6
SPLIT = 3
NORM_SLACK = 1.01
SKIP_LOG_MARGIN = 32.0
FAST_MAX_GAP = 60.0


def _split3(v):
    hi = v.astype(BF16)
    r1 = v - hi.astype(F32)
    mid = r1.astype(BF16)
    lo = (r1 - mid.astype(F32)).astype(BF16)
    return hi, mid, lo


def _softplus(y):
    return jnp.maximum(y, 0.0) + jnp.log1p(jnp.exp(-jnp.abs(y)))


def _silu(y):
    h = 0.5 * y
    return h + h * jnp.tanh(h)


def _layer_norm(r, g, b):
    mu = jnp.mean(r, axis=-1, keepdims=True)
    cen = r - mu
    var = jnp.mean(cen * cen, axis=-1, keepdims=True)
    return cen * lax.rsqrt(var + LN_EPS) * g + b


def _compiler_params(semantics):
    return pltpu.CompilerParams(dimension_semantics=semantics, vmem_limit_bytes=VMEM_LIMIT_BYTES)


def _ada_kernel(c_ref, w_ref, b_ref, o_ref):
    c = c_ref[...]
    cond = _silu(c)
    o_ref[0] = jnp.dot(cond, w_ref[0], precision=lax.Precision.HIGHEST,
                       preferred_element_type=F32) + b_ref[0]


def _ada_modulation(c, ada_w, ada_b):
    depth, d, n = ada_w.shape
    bsz = c.shape[0]
    tn = 1536
    out = pl.pallas_call(
        _ada_kernel,
        grid=(depth, n // tn),
        in_specs=[
            pl.BlockSpec((bsz, d), lambda i, j: (0, 0)),
            pl.BlockSpec((1, d, tn), lambda i, j: (i, 0, j)),
            pl.BlockSpec((1, 1, tn), lambda i, j: (i, 0, j)),
        ],
        out_specs=pl.BlockSpec((1, bsz, tn), lambda i, j: (i, 0, j)),
        out_shape=jax.ShapeDtypeStruct((depth, bsz, n), F32),
        compiler_params=_compiler_params(("arbitrary", "arbitrary")),
        name="ada_modulation",
    )(c, ada_w, ada_b.reshape(depth, 1, n))
    return out.reshape(depth, bsz, 6, d)


def _fox_inproj_kernel(x_ref, mod_ref, w_ref, bf_ref, tri_ref, eq_ref, ek_ref, oq_ref, ok_ref, hs_ref,
                       q_ref, k_ref, v_ref, qaug_ref, kaug_ref, rblk_ref, qn_ref, kn_ref,
                       carry_ref, kmax_ref, *, tm, gb):
    @pl.when(pl.program_id(1) == 0)
    def _():
        carry_ref[...] = jnp.zeros_like(carry_ref)
        kmax_ref[...] = jnp.zeros_like(kmax_ref)

    x = x_ref[0]
    u = (x * (1.0 + mod_ref[0, 1:2, :]) + mod_ref[0, 0:1, :]).astype(BF16)
    d = D_MODEL
    qb = (jnp.dot(u, w_ref[:, 0:d], preferred_element_type=F32) * (FOX_HEAD_DIM ** -0.5)).astype(BF16)
    kb = jnp.dot(u, w_ref[:, d:2 * d], preferred_element_type=F32).astype(BF16)
    q_ref[0] = qb
    k_ref[0] = kb

    def head_norms(t):
        t = t.astype(F32)
        return jnp.sqrt(jnp.dot((t * t).astype(BF16), hs_ref[...], preferred_element_type=F32)) * NORM_SLACK

    qnorm = head_norms(qb)
    kmax_ref[...] = jnp.maximum(kmax_ref[...], jnp.max(head_norms(kb), axis=0, keepdims=True))
    kn_ref[0, 0] = kmax_ref[...]
    v_ref[0] = jnp.dot(u, w_ref[:, 2 * d:3 * d], preferred_element_type=F32).astype(BF16)
    f = jnp.dot(u, w_ref[:, 3 * d:3 * d + FOX_HEADS], preferred_element_type=F32)
    f = jnp.concatenate([f, jnp.zeros((tm, LANES - FOX_HEADS), F32)], axis=1) + bf_ref[...]
    logf = jnp.minimum(f, 0.0) - jnp.log1p(jnp.exp(-jnp.abs(f)))

    for g in range(tm // gb):
        pieces, run = [], jnp.zeros((1, LANES), F32)
        for k in range(gb // PREFIX_ROWS):
            lf = logf[g * gb + k * PREFIX_ROWS:g * gb + (k + 1) * PREFIX_ROWS]
            part = jnp.dot(tri_ref[...], jnp.concatenate(_split3(lf), axis=0), preferred_element_type=F32) + run
            pieces.append(part)
            run = part[PREFIX_ROWS - 1:PREFIX_ROWS, :]
        rel = jnp.concatenate(pieces, axis=0)
        rcat = jnp.concatenate(_split3(rel), axis=1)
        qaug_ref[0, g * gb:(g + 1) * gb, :] = (
            jnp.dot(rcat, eq_ref[...], preferred_element_type=F32) + oq_ref[...]).astype(BF16)
        kaug_ref[0, g * gb:(g + 1) * gb, :] = (
            jnp.dot(rcat, ek_ref[...], preferred_element_type=F32) + ok_ref[...]).astype(BF16)
        rblk_ref[0, 0, g:g + 1, :] = carry_ref[...]
        qn_ref[0, 0, g:g + 1, :] = jnp.max(qnorm[g * gb:(g + 1) * gb], axis=0, keepdims=True)
        carry_ref[...] = carry_ref[...] + rel[gb - 1:gb, :]


def _fox_aug_constants():
    lane = jnp.arange(LANES)
    row = jnp.arange(SPLIT * LANES)
    piece, head = row // LANES, row % LANES
    valid = head < FOX_HEADS
    eq = (valid[:, None] & (lane[None, :] == (AUG_PER_HEAD * head + piece)[:, None])).astype(BF16)
    ek = -(valid[:, None] & (lane[None, :] == (AUG_PER_HEAD * head + SPLIT + piece)[:, None])).astype(BF16)
    used = lane < AUG_PER_HEAD * FOX_HEADS
    oq = (used & (lane % AUG_PER_HEAD >= SPLIT)).astype(F32)[None, :]
    ok = (used & (lane % AUG_PER_HEAD < SPLIT)).astype(F32)[None, :]
    t = jnp.arange(PREFIX_ROWS)
    tri = (t[:, None] >= t[None, :]).astype(BF16)
    tri3 = jnp.concatenate([tri] * SPLIT, axis=1)
    hs = (jnp.arange(D_MODEL)[:, None] // FOX_HEAD_DIM == lane[None, :]).astype(BF16)
    return tri3, eq, ek, oq, ok, hs


def _fox_inproj(x, mod, w_in, b_f):
    bsz, s, d = x.shape
    tm, gb = min(ROW_TILE, s), ATTN_BLOCK
    w = w_in.astype(BF16)
    bf = jnp.pad(b_f, (0, LANES - FOX_HEADS)).reshape(1, LANES).astype(F32)
    tri3, eq, ek, oq, ok, hs = _fox_aug_constants()
    const = lambda shape: pl.BlockSpec(shape, lambda b, i: (0,) * len(shape))
    rows = lambda width: pl.BlockSpec((1, tm, width), lambda b, i: (b, i, 0))
    per_block = pl.BlockSpec((1, 1, tm // gb, LANES), lambda b, i: (b, i, 0, 0))
    return pl.pallas_call(
        functools.partial(_fox_inproj_kernel, tm=tm, gb=gb),
        grid=(bsz, s // tm),
        in_specs=[rows(d), pl.BlockSpec((1, 6, d), lambda b, i: (b, 0, 0)), const(w.shape),
                  const(bf.shape), const(tri3.shape), const(eq.shape), const(ek.shape),
                  const(oq.shape), const(ok.shape), const(hs.shape)],
        out_specs=[rows(d), rows(d), rows(d), rows(LANES), rows(LANES), per_block, per_block,
                   pl.BlockSpec((1, 1, 1, LANES), lambda b, i: (b, i, 0, 0))],
        out_shape=[jax.ShapeDtypeStruct((bsz, s, d), BF16)] * 3
        + [jax.ShapeDtypeStruct((bsz, s, LANES), BF16)] * 2
        + [jax.ShapeDtypeStruct((bsz, s // tm, tm // gb, LANES), F32)] * 2
        + [jax.ShapeDtypeStruct((bsz, s // tm, 1, LANES), F32)],
        scratch_shapes=[pltpu.VMEM((1, LANES), F32), pltpu.VMEM((1, LANES), F32)],
        compiler_params=_compiler_params(("arbitrary", "arbitrary")),
        name="fox_inproj",
    )(x, mod, w, bf, tri3, eq, ek, oq, ok, hs)


def _fox_attn_kernel(r_ref, qn_ref, kn_ref, q_ref, qaug_ref, k_ref, v_ref, kaug_ref, o_ref, *, blk, nblk, npairs):
    b, hg, i = pl.program_id(0), pl.program_id(1), pl.program_id(2)
    nheads = 2 * npairs
    lane = lax.broadcasted_iota(jnp.int32, (1, LANES), 1)
    qa = qaug_ref[0]
    zero = jnp.zeros((), BF16)
    nt = (((1,), (1,)), ((), ()))
    slab = lambda pair: slice(pair * LANES, (pair + 1) * LANES)

    qcat, own, rbase, ubound, first = [], [], [], [], []
    for hh in range(nheads):
        h = nheads * hg + hh
        half = hh % 2
        own.append((lane >= half * FOX_HEAD_DIM) & (lane < (half + 1) * FOX_HEAD_DIM))
        aug = (lane >= AUG_PER_HEAD * h) & (lane < AUG_PER_HEAD * (h + 1))
        qcat.append(jnp.concatenate([jnp.where(own[hh], q_ref[0, :, slab(hh // 2)], zero),
                                     jnp.where(aug, qa, zero)], axis=1))
        base = (b * FOX_HEADS + h) * nblk
        rbase.append(base)
        ub = qn_ref[base + i] * kn_ref[b * FOX_HEADS + h]
        ubound.append(ub)
        thresh = -(SKIP_LOG_MARGIN + 2.0 * ub)
        r_i = r_ref[base + i]
        first.append(lax.while_loop(
            lambda j, base=base, r_i=r_i, thresh=thresh: (j > 0) & (r_i - r_ref[base + j] >= thresh),
            lambda j: j - 1, i))
    j_first = functools.reduce(jnp.minimum, first)
    fast = 2.0 * functools.reduce(jnp.maximum, ubound) <= FAST_MAX_GAP

    def block_shift(hh, j):
        return r_ref[rbase[hh] + i] - r_ref[rbase[hh] + j]

    def causal_mask(z):
        rr = lax.broadcasted_iota(jnp.int32, z.shape, 0)
        cc = lax.broadcasted_iota(jnp.int32, z.shape, 1)
        return jnp.where(rr >= cc, z, -jnp.inf)

    def key_rows(j):
        return pl.ds(pl.multiple_of(j * blk, blk), blk)

    def key_operand(j, pair):
        return jnp.concatenate([k_ref[0, key_rows(j), slab(pair)], kaug_ref[0, key_rows(j), :]], axis=1)

    @pl.when(fast)
    def _():
        ones_col = [jnp.broadcast_to(jnp.where(lane == half, 1.0, 0.0), (blk, LANES)).astype(BF16)
                    for half in range(2)]

        def tile(j, diagonal):
            outs = []
            for pair in range(npairs):
                kcat = key_operand(j, pair)
                vs = v_ref[0, key_rows(j), slab(pair)]
                out = None
                for half in range(2):
                    hh = 2 * pair + half
                    shift = block_shift(hh, j) - ubound[hh]
                    vaug = jnp.concatenate([jnp.where(own[hh], vs, zero), ones_col[half]], axis=1)
                    if diagonal:
                        hb = blk // 2
                        zl = lax.dot_general(qcat[hh], kcat[:hb], nt, preferred_element_type=F32) + shift
                        zr = lax.dot_general(qcat[hh][hb:], kcat[hb:], nt, preferred_element_type=F32) + shift
                        part = jnp.dot(jnp.exp(causal_mask(zl)).astype(BF16), vaug[:hb],
                                       preferred_element_type=F32)
                        lower = jnp.dot(jnp.exp(causal_mask(zr)).astype(BF16), vaug[hb:],
                                        preferred_element_type=F32)
                        part = jnp.concatenate([part[:hb], part[hb:] + lower], axis=0)
                    else:
                        z = lax.dot_general(qcat[hh], kcat, nt, preferred_element_type=F32) + shift
                        part = jnp.dot(jnp.exp(z).astype(BF16), vaug, preferred_element_type=F32)
                    out = part if out is None else out + part
                outs.append(out)
            return tuple(outs)

        def add(accs, parts):
            return tuple(a + p for a, p in zip(accs, parts))

        def finish(accs):
            for pair, acc in enumerate(accs):
                l = jnp.where(own[0], acc[:, LANES:LANES + 1], acc[:, LANES + 1:LANES + 2])
                o_ref[0, :, slab(pair)] = (acc[:, :LANES] / l).astype(o_ref.dtype)

        @pl.when(j_first < i)
        def _():
            init = tuple(jnp.zeros((blk, 2 * LANES), F32) for _ in range(npairs))
            accs = lax.fori_loop(j_first, i - 1, lambda j, a: add(a, tile(j, False)), init)
            finish(add(add(accs, tile(i - 1, False)), tile(i, True)))

        @pl.when(j_first >= i)
        def _():
            finish(tile(i, True))

    @pl.when(jnp.logical_not(fast))
    def _():
        def tile(j, carry, diagonal):
            out = []
            for hh in range(nheads):
                m, l, acc = carry[hh]
                z = lax.dot_general(qcat[hh], key_operand(j, hh // 2), nt, preferred_element_type=F32)
                dij = block_shift(hh, j)
                if diagonal:
                    z = causal_mask(z)
                m_new = jnp.maximum(m, jnp.max(z, axis=1, keepdims=True) + dij)
                p = jnp.exp(z - (m_new - dij))
                alpha = jnp.exp(m - m_new)
                l = alpha * l + jnp.sum(p, axis=1, keepdims=True)
                vm = jnp.where(own[hh], v_ref[0, key_rows(j), slab(hh // 2)], zero)
                acc = alpha * acc + jnp.dot(p.astype(BF16), vm, preferred_element_type=F32)
                out.append((m_new, l, acc))
            return tuple(out)

        init = tuple((jnp.full((blk, 1), -jnp.inf, F32), jnp.zeros((blk, 1), F32),
                      jnp.zeros((blk, LANES), F32)) for _ in range(nheads))
        carry = lax.fori_loop(j_first, i, lambda j, c: tile(j, c, False), init)
        carry = tile(i, carry, True)
        for pair in range(npairs):
            (_, l0, a0), (_, l1, a1) = carry[2 * pair], carry[2 * pair + 1]
            o_ref[0, :, slab(pair)] = (a0 / l0 + a1 / l1).astype(o_ref.dtype)


def _fox_attention(q, k, v, qaug, kaug, rblk, qnblk, kntile):
    bsz, s, d = q.shape
    blk = ATTN_BLOCK
    nblk = s // blk
    npairs = ATTN_PAIRS_PER_STEP
    width = npairs * LANES
    flat = lambda a: a.reshape(bsz, nblk, LANES)[:, :, :FOX_HEADS].transpose(0, 2, 1).reshape(-1)
    knflat = kntile[:, -1, 0, :FOX_HEADS].reshape(-1)
    idx = lambda f: (lambda b, hg, i, r, qn, kn: f(b, hg, i))
    grid_spec = pltpu.PrefetchScalarGridSpec(
        num_scalar_prefetch=3,
        grid=(bsz, FOX_HEADS // (2 * npairs), nblk),
        in_specs=[
            pl.BlockSpec((1, blk, width), idx(lambda b, hg, i: (b, i, hg))),
            pl.BlockSpec((1, blk, LANES), idx(lambda b, hg, i: (b, i, 0))),
            pl.BlockSpec((1, s, width), idx(lambda b, hg, i: (b, 0, hg))),
            pl.BlockSpec((1, s, width), idx(lambda b, hg, i: (b, 0, hg))),
            pl.BlockSpec((1, s, LANES), idx(lambda b, hg, i: (b, 0, 0))),
        ],
        out_specs=pl.BlockSpec((1, blk, width), idx(lambda b, hg, i: (b, i, hg))),
    )
    return pl.pallas_call(
        functools.partial(_fox_attn_kernel, blk=blk, nblk=nblk, npairs=npairs),
        grid_spec=grid_spec,
        out_shape=jax.ShapeDtypeStruct((bsz, s, d), BF16),
        compiler_params=_compiler_params(("arbitrary", "arbitrary", "arbitrary")),
        name="fox_attention",
    )(flat(rblk), flat(qnblk), knflat, q, qaug, k, v, kaug)


def _proj_mlp_kernel(a_ref, wo_ref, x_ref, mod_ref, g1_ref, b1_ref, w1_ref, w2_ref, g2_ref, b2_ref, o_ref,
                     *, ff_chunk):
    y = jnp.dot(a_ref[0], wo_ref[...], preferred_element_type=F32)
    x1 = _layer_norm(RESIDUAL_ALPHA * x_ref[0] + (1.0 + mod_ref[0, 2:3, :]) * y, g1_ref[...], b1_ref[...])
    u = (x1 * (1.0 + mod_ref[0, 4:5, :]) + mod_ref[0, 3:4, :]).astype(BF16)
    y = None
    for c in range(D_FF // ff_chunk):
        h = jnp.dot(u, w1_ref[:, c * ff_chunk:(c + 1) * ff_chunk], preferred_element_type=F32)
        h = jnp.square(jnp.maximum(h, 0.0)).astype(BF16)
        part = jnp.dot(h, w2_ref[c * ff_chunk:(c + 1) * ff_chunk, :], preferred_element_type=F32)
        y = part if y is None else y + part
    r = RESIDUAL_ALPHA * x1 + (1.0 + mod_ref[0, 5:6, :]) * y
    o_ref[0] = _layer_norm(r, g2_ref[...], b2_ref[...])


def _proj_mlp(a, w_o, x, mod, ln1_g, ln1_b, w1, w2, ln2_g, ln2_b):
    bsz, s, d = x.shape
    kdim = a.shape[-1]
    tm = min(ROW_TILE, s)
    rows = lambda width: pl.BlockSpec((1, tm, width), lambda b, i: (b, i, 0))
    resident = lambda shape: pl.BlockSpec(shape, lambda b, i: (0,) * len(shape), pipeline_mode=pl.Buffered(1))
    vec = pl.BlockSpec((1, d), lambda b, i: (0, 0))
    return pl.pallas_call(
        functools.partial(_proj_mlp_kernel, ff_chunk=1024),
        grid=(bsz, s // tm),
        in_specs=[rows(kdim), resident((kdim, d)), rows(d), pl.BlockSpec((1, 6, d), lambda b, i: (b, 0, 0)),
                  vec, vec, resident((d, D_FF)), resident((D_FF, d)), vec, vec],
        out_specs=rows(d),
        out_shape=jax.ShapeDtypeStruct((bsz, s, d), F32),
        compiler_params=_compiler_params(("arbitrary", "arbitrary")),
        name="proj_mlp",
    )(a, w_o.astype(BF16), x, mod, ln1_g.reshape(1, d), ln1_b.reshape(1, d), w1.astype(BF16), w2.astype(BF16),
      ln2_g.reshape(1, d), ln2_b.reshape(1, d))


def _ssd_inproj_kernel(x_ref, mod_ref, w_ref, convw_ref, convb_ref, z_ref, xs_ref, bc_ref, dt_ref, ext_ref,
                       *, tm, chunk):
    halo = SUBLANES

    @pl.when(pl.program_id(1) == 0)
    def _():
        ext_ref[0:halo, :] = jnp.zeros((halo, SSM_CONV_DIM), F32)

    u = (x_ref[0] * (1.0 + mod_ref[0, 1:2, :]) + mod_ref[0, 0:1, :]).astype(BF16)
    for c in range(SSM_D_INNER // chunk):
        z_ref[0, :, c * chunk:(c + 1) * chunk] = jnp.dot(
            u, w_ref[:, c * chunk:(c + 1) * chunk], preferred_element_type=F32)
    for c in range(SSM_CONV_DIM // chunk):
        cols = slice(c * chunk, (c + 1) * chunk)
        ext_ref[halo:halo + tm, cols] = jnp.dot(
            u, w_ref[:, SSM_D_INNER + c * chunk:SSM_D_INNER + (c + 1) * chunk], preferred_element_type=F32)
        ext = ext_ref[:, cols]
        ext1 = pltpu.roll(ext, 1, axis=0)
        near = ext * convw_ref[3:4, cols] + ext1 * convw_ref[2:3, cols]
        far = ext * convw_ref[1:2, cols] + ext1 * convw_ref[0:1, cols]
        conv = (near + pltpu.roll(far, 2, axis=0))[halo:halo + tm] + convb_ref[:, cols]
        ext_ref[0:halo, cols] = ext_ref[tm:tm + halo, cols]
        act = _silu(conv)
        if (c + 1) * chunk <= SSM_D_INNER:
            xs_ref[0, :, cols] = act
        else:
            bc_ref[0, :, c * chunk - SSM_D_INNER:(c + 1) * chunk - SSM_D_INNER] = act.astype(BF16)
    off = SSM_D_INNER + SSM_CONV_DIM
    dt = jnp.dot(u, w_ref[:, off:off + SSM_HEADS], preferred_element_type=F32)
    dt_ref[0] = jnp.concatenate([dt, jnp.zeros((tm, LANES - SSM_HEADS), F32)], axis=1)


def _ssd_inproj(x, mod, w_in, conv_w, conv_b):
    bsz, s, d = x.shape
    tm = min(ROW_TILE, s)
    w = w_in.astype(BF16)
    rows = lambda width: pl.BlockSpec((1, tm, width), lambda b, i: (b, i, 0))
    const = lambda shape: pl.BlockSpec(shape, lambda b, i: (0,) * len(shape))
    return pl.pallas_call(
        functools.partial(_ssd_inproj_kernel, tm=tm, chunk=1024),
        grid=(bsz, s // tm),
        in_specs=[rows(d), pl.BlockSpec((1, 6, d), lambda b, i: (b, 0, 0)),
                  pl.BlockSpec(w.shape, lambda b, i: (0, 0), pipeline_mode=pl.Buffered(1)),
                  const((SSM_CONV, SSM_CONV_DIM)), const((1, SSM_CONV_DIM))],
        out_specs=[rows(SSM_D_INNER), rows(SSM_D_INNER), rows(2 * SSM_BC_WIDTH), rows(LANES)],
        out_shape=[jax.ShapeDtypeStruct((bsz, s, SSM_D_INNER), F32),
                   jax.ShapeDtypeStruct((bsz, s, SSM_D_INNER), F32),
                   jax.ShapeDtypeStruct((bsz, s, 2 * SSM_BC_WIDTH), BF16),
                   jax.ShapeDtypeStruct((bsz, s, LANES), F32)],
        scratch_shapes=[pltpu.VMEM((SUBLANES + tm, SSM_CONV_DIM), F32)],
        compiler_params=_compiler_params(("arbitrary", "arbitrary")),
        name="ssd_inproj",
    )(x, mod, w, conv_w.astype(F32), conv_b.reshape(1, -1).astype(F32))


def _ssd_core_kernel(xs_ref, bc_ref, z_ref, dt_ref, dtb_ref, alog_ref, dskip_ref,
                     normw_ref, tri_ref, xp_ref,
                     o_ref, state_ref, *, nchunks):
    ln = SSM_CHUNK

    @pl.when(pl.program_id(1) == 0)
    def _():
        state_ref[...] = jnp.zeros_like(state_ref)

    for chunk in range(nchunks):
        _ssd_chunk(slice(chunk * ln, (chunk + 1) * ln), xs_ref, bc_ref, z_ref, dt_ref, dtb_ref, alog_ref,
                   dskip_ref, normw_ref, tri_ref, xp_ref, o_ref, state_ref)


def _ssd_chunk(rows, xs_ref, bc_ref, z_ref, dt_ref, dtb_ref, alog_ref, dskip_ref, normw_ref, tri_ref,
               xp_ref, o_ref, state_ref):
    ln = SSM_CHUNK
    xs = xs_ref[0, rows, :]

    dt = _softplus(dt_ref[0, rows, :] + dtb_ref[...])
    dt_rows = dt.T
    dta = dt * (-jnp.exp(alog_ref[...]))
    acs = jnp.dot(tri_ref[...], jnp.concatenate(_split3(dta), axis=0), preferred_element_type=F32)
    acs_last = acs[ln - 1:ln, :]
    dtdte = dt * jnp.exp(acs_last - acs)
    acs_rows = acs.T

    def expand(v):
        return jnp.dot(jnp.concatenate(_split3(v), axis=1), xp_ref[...], preferred_element_type=F32)

    xb = xs.astype(BF16)
    xdte = (xs * expand(dtdte)).astype(BF16)
    eacs_x = expand(jnp.exp(acs))

    rr = lax.broadcasted_iota(jnp.int32, (ln, ln), 0)
    cc = lax.broadcasted_iota(jnp.int32, (ln, ln), 1)
    causal = rr >= cc
    lane_grp = lax.broadcasted_iota(jnp.int32, (1, SSM_GROUP_WIDTH), 1)
    zero = jnp.zeros((), BF16)
    nt = (((1,), (1,)), ((), ()))
    tn = (((0,), (0,)), ((), ()))

    for g in range(SSM_GROUPS):
        gs = slice(g * SSM_GROUP_WIDTH, (g + 1) * SSM_GROUP_WIDTH)
        bg = bc_ref[0, rows, g * SSM_STATE:(g + 1) * SSM_STATE]
        cg = bc_ref[0, rows, SSM_BC_WIDTH + g * SSM_STATE:SSM_BC_WIDTH + (g + 1) * SSM_STATE]
        cb = lax.dot_general(cg, bg, nt, preferred_element_type=F32)
        xg = xb[:, gs]
        y = None
        for r in range(SSM_HEADS_PER_GROUP):
            h = g * SSM_HEADS_PER_GROUP + r
            seg = acs[:, h:h + 1] - acs_rows[h:h + 1, :]
            decay = jnp.exp(jnp.where(causal, seg, -jnp.inf))
            mix = (cb * decay * dt_rows[h:h + 1, :]).astype(BF16)
            own = (lane_grp >= r * SSM_HEAD_DIM) & (lane_grp < (r + 1) * SSM_HEAD_DIM)
            part = jnp.dot(mix, jnp.where(own, xg, zero), preferred_element_type=F32)
            y = part if y is None else y + part
        prev = state_ref[g]
        y = y + jnp.dot(cg, prev.astype(BF16), preferred_element_type=F32) * eacs_x[:, gs]
        y = y + xs[:, gs] * dskip_ref[:, gs]
        new_states = lax.dot_general(bg, xdte[:, gs], tn, preferred_element_type=F32)
        state_ref[g] = prev * eacs_x[ln - 1:ln, gs] + new_states
        zg = z_ref[0, rows, gs]
        gated = y * _silu(zg)
        ms = jnp.mean(gated * gated, axis=-1, keepdims=True)
        o_ref[0, rows, gs] = (gated * lax.rsqrt(ms + RMS_EPS) * normw_ref[:, gs]).astype(o_ref.dtype)


def _ssd_constants():
    ln = SSM_CHUNK
    t = jnp.arange(ln)
    tri = (t[:, None] >= t[None, :]).astype(BF16)
    tri3 = jnp.concatenate([tri] * SPLIT, axis=1)
    head = jnp.arange(SPLIT * LANES) % LANES
    col = jnp.arange(SSM_D_INNER)
    xp = ((head < SSM_HEADS)[:, None] & (head[:, None] == (col // SSM_HEAD_DIM)[None, :])).astype(BF16)
    return tri3, xp


def _ssd_core(z, xs, bc, dt, dt_bias, a_log, d_skip, norm_w):
    bsz, s, _ = z.shape
    nchunks = SSD_CHUNKS_PER_STEP
    ln = nchunks * SSM_CHUNK
    pad_heads = lambda v: jnp.pad(v.astype(F32), (0, LANES - SSM_HEADS)).reshape(1, LANES)
    consts = _ssd_constants()
    const = lambda a: pl.BlockSpec(a.shape, lambda b, i: (0,) * a.ndim)
    rows = lambda width: pl.BlockSpec((1, ln, width), lambda b, i: (b, i, 0))
    small = [pad_heads(dt_bias), pad_heads(a_log),
             jnp.repeat(d_skip.astype(F32), SSM_HEAD_DIM).reshape(1, -1), norm_w.reshape(1, -1).astype(F32)]
    return pl.pallas_call(
        functools.partial(_ssd_core_kernel, nchunks=nchunks),
        grid=(bsz, s // ln),
        in_specs=[rows(SSM_D_INNER), rows(2 * SSM_BC_WIDTH), rows(SSM_D_INNER), rows(LANES)]
        + [const(a) for a in small] + [const(a) for a in consts],
        out_specs=rows(SSM_D_INNER),
        out_shape=jax.ShapeDtypeStruct((bsz, s, SSM_D_INNER), BF16),
        scratch_shapes=[pltpu.VMEM((SSM_GROUPS, SSM_STATE, SSM_GROUP_WIDTH), F32)],
        compiler_params=_compiler_params(("arbitrary", "arbitrary")),
        name="ssd_core",
    )(xs, bc, z, dt, *small, *consts)


def kernel(x, c, ada_w, ada_b, ln_mix_g, ln_mix_b, ln_mlp_g, ln_mlp_b, mlp_w1, mlp_w2, fox_w_in, fox_b_f, fox_w_o, ssm_w_in, ssm_conv_w, ssm_conv_b, ssm_dt_bias, ssm_a_log, ssm_d, ssm_norm_w, ssm_w_out):
    mod = _ada_modulation(c, ada_w, ada_b)

    q, k, v, qaug, kaug, rblk, qnblk, kntile = _fox_inproj(x, mod[0], fox_w_in[0], fox_b_f[0])
    attn = _fox_attention(q, k, v, qaug, kaug, rblk, qnblk, kntile)
    x = _proj_mlp(attn, fox_w_o[0], x, mod[0], ln_mix_g[0], ln_mix_b[0], mlp_w1[0], mlp_w2[0],
                  ln_mlp_g[0], ln_mlp_b[0])

    z, xs, bc, dt = _ssd_inproj(x, mod[1], ssm_w_in[0], ssm_conv_w[0], ssm_conv_b[0])
    y = _ssd_core(z, xs, bc, dt, ssm_dt_bias[0], ssm_a_log[0], ssm_d[0], ssm_norm_w[0])
    x = _proj_mlp(y, ssm_w_out[0], x, mod[1], ln_mix_g[1], ln_mix_b[1], mlp_w1[1], mlp_w2[1],
                  ln_mlp_g[1], ln_mlp_b[1])
    return x
```

```python
import functools

import jax
import jax.numpy as jnp
from jax import lax
from jax.experimental import pallas as pl
from jax.experimental.pallas import tpu as pltpu

F32 = jnp.float32
BF16 = jnp.bfloat16

D_MODEL = 1024
DEPTH = 2
FOX_HEADS = 16
FOX_HEAD_DIM = D_MODEL // FOX_HEADS
SSM_D_INNER = 2 * D_MODEL
SSM_HEAD_DIM = 64
SSM_HEADS = SSM_D_INNER // SSM_HEAD_DIM
SSM_GROUPS = 8
SSM_HEADS_PER_GROUP = SSM_HEADS // SSM_GROUPS
SSM_STATE = 128
SSM_CONV = 4
SSM_CHUNK = 128
SSM_GROUP_WIDTH = SSM_D_INNER // SSM_GROUPS
SSM_BC_WIDTH = SSM_GROUPS * SSM_STATE
SSM_CONV_DIM = SSM_D_INNER + 2 * SSM_BC_WIDTH
D_FF = 4 * D_MODEL
LN_EPS = 1e-5
RMS_EPS = 1e-5
RESIDUAL_ALPHA = (2.0 * DEPTH) ** 0.25

LANES = 128
SUBLANES = 8
VMEM_LIMIT_BYTES = 56 * 1024 * 1024

ATTN_BLOCK = 512
ATTN_PAIRS_PER_STEP = 2
PREFIX_ROWS = 128
ROW_TILE = 512
SSD_CHUNKS_PER_STEP = 4
AUG_PER_HEAD = 6
SPLIT = 3
NORM_SLACK = 1.01
SKIP_LOG_MARGIN = 32.0
FAST_MAX_GAP = 60.0


def _split3(v):
    hi = v.astype(BF16)
    r1 = v - hi.astype(F32)
    mid = r1.astype(BF16)
    lo = (r1 - mid.astype(F32)).astype(BF16)
    return hi, mid, lo


def _softplus(y):
    return jnp.maximum(y, 0.0) + jnp.log1p(jnp.exp(-jnp.abs(y)))


def _silu(y):
    h = 0.5 * y
    return h + h * jnp.tanh(h)


def _layer_norm(r, g, b):
    mu = jnp.mean(r, axis=-1, keepdims=True)
    cen = r - mu
    var = jnp.mean(cen * cen, axis=-1, keepdims=True)
    return cen * lax.rsqrt(var + LN_EPS) * g + b


def _compiler_params(semantics):
    return pltpu.CompilerParams(dimension_semantics=semantics, vmem_limit_bytes=VMEM_LIMIT_BYTES)


def _ada_kernel(c_ref, w_ref, b_ref, o_ref):
    c = c_ref[...]
    cond = _silu(c)
    o_ref[0] = jnp.dot(cond, w_ref[0], precision=lax.Precision.HIGHEST,
                       preferred_element_type=F32) + b_ref[0]


def _ada_modulation(c, ada_w, ada_b):
    depth, d, n = ada_w.shape
    bsz = c.shape[0]
    tn = 1536
    out = pl.pallas_call(
        _ada_kernel,
        grid=(depth, n // tn),
        in_specs=[
            pl.BlockSpec((bsz, d), lambda i, j: (0, 0)),
            pl.BlockSpec((1, d, tn), lambda i, j: (i, 0, j)),
            pl.BlockSpec((1, 1, tn), lambda i, j: (i, 0, j)),
        ],
        out_specs=pl.BlockSpec((1, bsz, tn), lambda i, j: (i, 0, j)),
        out_shape=jax.ShapeDtypeStruct((depth, bsz, n), F32),
        compiler_params=_compiler_params(("arbitrary", "arbitrary")),
        name="ada_modulation",
    )(c, ada_w, ada_b.reshape(depth, 1, n))
    return out.reshape(depth, bsz, 6, d)


def _fox_inproj_kernel(x_ref, mod_ref, w_ref, bf_ref, tri_ref, eq_ref, ek_ref, oq_ref, ok_ref, hs_ref,
                       q_ref, k_ref, v_ref, qaug_ref, kaug_ref, rblk_ref, qn_ref, kn_ref,
                       carry_ref, kmax_ref, *, tm, gb):
    @pl.when(pl.program_id(1) == 0)
    def _():
        carry_ref[...] = jnp.zeros_like(carry_ref)
        kmax_ref[...] = jnp.zeros_like(kmax_ref)

    x = x_ref[0]
    u = (x * (1.0 + mod_ref[0, 1:2, :]) + mod_ref[0, 0:1, :]).astype(BF16)
    d = D_MODEL
    qb = (jnp.dot(u, w_ref[:, 0:d], preferred_element_type=F32) * (FOX_HEAD_DIM ** -0.5)).astype(BF16)
    kb = jnp.dot(u, w_ref[:, d:2 * d], preferred_element_type=F32).astype(BF16)
    q_ref[0] = qb
    k_ref[0] = kb

    def head_norms(t):
        t = t.astype(F32)
        return jnp.sqrt(jnp.dot((t * t).astype(BF16), hs_ref[...], preferred_element_type=F32)) * NORM_SLACK

    qnorm = head_norms(qb)
    kmax_ref[...] = jnp.maximum(kmax_ref[...], jnp.max(head_norms(kb), axis=0, keepdims=True))
    kn_ref[0, 0] = kmax_ref[...]
    v_ref[0] = jnp.dot(u, w_ref[:, 2 * d:3 * d], preferred_element_type=F32).astype(BF16)
    f = jnp.dot(u, w_ref[:, 3 * d:3 * d + FOX_HEADS], preferred_element_type=F32)
    f = jnp.concatenate([f, jnp.zeros((tm, LANES - FOX_HEADS), F32)], axis=1) + bf_ref[...]
    logf = jnp.minimum(f, 0.0) - jnp.log1p(jnp.exp(-jnp.abs(f)))

    for g in range(tm // gb):
        pieces, run = [], jnp.zeros((1, LANES), F32)
        for k in range(gb // PREFIX_ROWS):
            lf = logf[g * gb + k * PREFIX_ROWS:g * gb + (k + 1) * PREFIX_ROWS]
            part = jnp.dot(tri_ref[...], jnp.concatenate(_split3(lf), axis=0), preferred_element_type=F32) + run
            pieces.append(part)
            run = part[PREFIX_ROWS - 1:PREFIX_ROWS, :]
        rel = jnp.concatenate(pieces, axis=0)
        rcat = jnp.concatenate(_split3(rel), axis=1)
        qaug_ref[0, g * gb:(g + 1) * gb, :] = (
            jnp.dot(rcat, eq_ref[...], preferred_element_type=F32) + oq_ref[...]).astype(BF16)
        kaug_ref[0, g * gb:(g + 1) * gb, :] = (
            jnp.dot(rcat, ek_ref[...], preferred_element_type=F32) + ok_ref[...]).astype(BF16)
        rblk_ref[0, 0, g:g + 1, :] = carry_ref[...]
        qn_ref[0, 0, g:g + 1, :] = jnp.max(qnorm[g * gb:(g + 1) * gb], axis=0, keepdims=True)
        carry_ref[...] = carry_ref[...] + rel[gb - 1:gb, :]


def _fox_aug_constants():
    lane = jnp.arange(LANES)
    row = jnp.arange(SPLIT * LANES)
    piece, head = row // LANES, row % LANES
    valid = head < FOX_HEADS
    eq = (valid[:, None] & (lane[None, :] == (AUG_PER_HEAD * head + piece)[:, None])).astype(BF16)
    ek = -(valid[:, None] & (lane[None, :] == (AUG_PER_HEAD * head + SPLIT + piece)[:, None])).astype(BF16)
    used = lane < AUG_PER_HEAD * FOX_HEADS
    oq = (used & (lane % AUG_PER_HEAD >= SPLIT)).astype(F32)[None, :]
    ok = (used & (lane % AUG_PER_HEAD < SPLIT)).astype(F32)[None, :]
    t = jnp.arange(PREFIX_ROWS)
    tri = (t[:, None] >= t[None, :]).astype(BF16)
    tri3 = jnp.concatenate([tri] * SPLIT, axis=1)
    hs = (jnp.arange(D_MODEL)[:, None] // FOX_HEAD_DIM == lane[None, :]).astype(BF16)
    return tri3, eq, ek, oq, ok, hs


def _fox_inproj(x, mod, w_in, b_f):
    bsz, s, d = x.shape
    tm, gb = min(ROW_TILE, s), ATTN_BLOCK
    w = w_in.astype(BF16)
    bf = jnp.pad(b_f, (0, LANES - FOX_HEADS)).reshape(1, LANES).astype(F32)
    tri3, eq, ek, oq, ok, hs = _fox_aug_constants()
    const = lambda shape: pl.BlockSpec(shape, lambda b, i: (0,) * len(shape))
    rows = lambda width: pl.BlockSpec((1, tm, width), lambda b, i: (b, i, 0))
    per_block = pl.BlockSpec((1, 1, tm // gb, LANES), lambda b, i: (b, i, 0, 0))
    return pl.pallas_call(
        functools.partial(_fox_inproj_kernel, tm=tm, gb=gb),
        grid=(bsz, s // tm),
        in_specs=[rows(d), pl.BlockSpec((1, 6, d), lambda b, i: (b, 0, 0)), const(w.shape),
                  const(bf.shape), const(tri3.shape), const(eq.shape), const(ek.shape),
                  const(oq.shape), const(ok.shape), const(hs.shape)],
        out_specs=[rows(d), rows(d), rows(d), rows(LANES), rows(LANES), per_block, per_block,
                   pl.BlockSpec((1, 1, 1, LANES), lambda b, i: (b, i, 0, 0))],
        out_shape=[jax.ShapeDtypeStruct((bsz, s, d), BF16)] * 3
        + [jax.ShapeDtypeStruct((bsz, s, LANES), BF16)] * 2
        + [jax.ShapeDtypeStruct((bsz, s // tm, tm // gb, LANES), F32)] * 2
        + [jax.ShapeDtypeStruct((bsz, s // tm, 1, LANES), F32)],
        scratch_shapes=[pltpu.VMEM((1, LANES), F32), pltpu.VMEM((1, LANES), F32)],
        compiler_params=_compiler_params(("arbitrary", "arbitrary")),
        name="fox_inproj",
    )(x, mod, w, bf, tri3, eq, ek, oq, ok, hs)


def _fox_attn_kernel(r_ref, qn_ref, kn_ref, q_ref, qaug_ref, k_ref, v_ref, kaug_ref, o_ref, *, blk, nblk, npairs):
    b, hg, i = pl.program_id(0), pl.program_id(1), pl.program_id(2)
    nheads = 2 * npairs
    lane = lax.broadcasted_iota(jnp.int32, (1, LANES), 1)
    qa = qaug_ref[0]
    zero = jnp.zeros((), BF16)
    nt = (((1,), (1,)), ((), ()))
    slab = lambda pair: slice(pair * LANES, (pair + 1) * LANES)

    qcat, own, rbase, ubound, first = [], [], [], [], []
    for hh in range(nheads):
        h = nheads * hg + hh
        half = hh % 2
        own.append((lane >= half * FOX_HEAD_DIM) & (lane < (half + 1) * FOX_HEAD_DIM))
        aug = (lane >= AUG_PER_HEAD * h) & (lane < AUG_PER_HEAD * (h + 1))
        qcat.append(jnp.concatenate([jnp.where(own[hh], q_ref[0, :, slab(hh // 2)], zero),
                                     jnp.where(aug, qa, zero)], axis=1))
        base = (b * FOX_HEADS + h) * nblk
        rbase.append(base)
        ub = qn_ref[base + i] * kn_ref[b * FOX_HEADS + h]
        ubound.append(ub)
        thresh = -(SKIP_LOG_MARGIN + 2.0 * ub)
        r_i = r_ref[base + i]
        first.append(lax.while_loop(
            lambda j, base=base, r_i=r_i, thresh=thresh: (j > 0) & (r_i - r_ref[base + j] >= thresh),
            lambda j: j - 1, i))
    j_first = functools.reduce(jnp.minimum, first)
    fast = 2.0 * functools.reduce(jnp.maximum, ubound) <= FAST_MAX_GAP

    def block_shift(hh, j):
        return r_ref[rbase[hh] + i] - r_ref[rbase[hh] + j]

    def causal_mask(z):
        rr = lax.broadcasted_iota(jnp.int32, z.shape, 0)
        cc = lax.broadcasted_iota(jnp.int32, z.shape, 1)
        return jnp.where(rr >= cc, z, -jnp.inf)

    def key_rows(j):
        return pl.ds(pl.multiple_of(j * blk, blk), blk)

    def key_operand(j, pair):
        return jnp.concatenate([k_ref[0, key_rows(j), slab(pair)], kaug_ref[0, key_rows(j), :]], axis=1)

    @pl.when(fast)
    def _():
        ones_col = [jnp.broadcast_to(jnp.where(lane == half, 1.0, 0.0), (blk, LANES)).astype(BF16)
                    for half in range(2)]

        def tile(j, diagonal):
            outs = []
            for pair in range(npairs):
                kcat = key_operand(j, pair)
                vs = v_ref[0, key_rows(j), slab(pair)]
                out = None
                for half in range(2):
                    hh = 2 * pair + half
                    shift = block_shift(hh, j) - ubound[hh]
                    vaug = jnp.concatenate([jnp.where(own[hh], vs, zero), ones_col[half]], axis=1)
                    if diagonal:
                        hb = blk // 2
                        zl = lax.dot_general(qcat[hh], kcat[:hb], nt, preferred_element_type=F32) + shift
                        zr = lax.dot_general(qcat[hh][hb:], kcat[hb:], nt, preferred_element_type=F32) + shift
                        part = jnp.dot(jnp.exp(causal_mask(zl)).astype(BF16), vaug[:hb],
                                       preferred_element_type=F32)
                        lower = jnp.dot(jnp.exp(causal_mask(zr)).astype(BF16), vaug[hb:],
                                        preferred_element_type=F32)
                        part = jnp.concatenate([part[:hb], part[hb:] + lower], axis=0)
                    else:
                        z = lax.dot_general(qcat[hh], kcat, nt, preferred_element_type=F32) + shift
                        part = jnp.dot(jnp.exp(z).astype(BF16), vaug, preferred_element_type=F32)
                    out = part if out is None else out + part
                outs.append(out)
            return tuple(outs)

        def add(accs, parts):
            return tuple(a + p for a, p in zip(accs, parts))

        def finish(accs):
            for pair, acc in enumerate(accs):
                l = jnp.where(own[0], acc[:, LANES:LANES + 1], acc[:, LANES + 1:LANES + 2])
                o_ref[0, :, slab(pair)] = (acc[:, :LANES] / l).astype(o_ref.dtype)

        @pl.when(j_first < i)
        def _():
            init = tuple(jnp.zeros((blk, 2 * LANES), F32) for _ in range(npairs))
            accs = lax.fori_loop(j_first, i - 1, lambda j, a: add(a, tile(j, False)), init)
            finish(add(add(accs, tile(i - 1, False)), tile(i, True)))

        @pl.when(j_first >= i)
        def _():
            finish(tile(i, True))

    @pl.when(jnp.logical_not(fast))
    def _():
        def tile(j, carry, diagonal):
            out = []
            for hh in range(nheads):
                m, l, acc = carry[hh]
                z = lax.dot_general(qcat[hh], key_operand(j, hh // 2), nt, preferred_element_type=F32)
                dij = block_shift(hh, j)
                if diagonal:
                    z = causal_mask(z)
                m_new = jnp.maximum(m, jnp.max(z, axis=1, keepdims=True) + dij)
                p = jnp.exp(z - (m_new - dij))
                alpha = jnp.exp(m - m_new)
                l = alpha * l + jnp.sum(p, axis=1, keepdims=True)
                vm = jnp.where(own[hh], v_ref[0, key_rows(j), slab(hh // 2)], zero)
                acc = alpha * acc + jnp.dot(p.astype(BF16), vm, preferred_element_type=F32)
                out.append((m_new, l, acc))
            return tuple(out)

        init = tuple((jnp.full((blk, 1), -jnp.inf, F32), jnp.zeros((blk, 1), F32),
                      jnp.zeros((blk, LANES), F32)) for _ in range(nheads))
        carry = lax.fori_loop(j_first, i, lambda j, c: tile(j, c, False), init)
        carry = tile(i, carry, True)
        for pair in range(npairs):
            (_, l0, a0), (_, l1, a1) = carry[2 * pair], carry[2 * pair + 1]
            o_ref[0, :, slab(pair)] = (a0 / l0 + a1 / l1).astype(o_ref.dtype)


def _fox_attention(q, k, v, qaug, kaug, rblk, qnblk, kntile):
    bsz, s, d = q.shape
    blk = ATTN_BLOCK
    nblk = s // blk
    npairs = ATTN_PAIRS_PER_STEP
    width = npairs * LANES
    flat = lambda a: a.reshape(bsz, nblk, LANES)[:, :, :FOX_HEADS].transpose(0, 2, 1).reshape(-1)
    knflat = kntile[:, -1, 0, :FOX_HEADS].reshape(-1)
    idx = lambda f: (lambda b, hg, i, r, qn, kn: f(b, hg, i))
    grid_spec = pltpu.PrefetchScalarGridSpec(
        num_scalar_prefetch=3,
        grid=(bsz, FOX_HEADS // (2 * npairs), nblk),
        in_specs=[
            pl.BlockSpec((1, blk, width), idx(lambda b, hg, i: (b, i, hg))),
            pl.BlockSpec((1, blk, LANES), idx(lambda b, hg, i: (b, i, 0))),
            pl.BlockSpec((1, s, width), idx(lambda b, hg, i: (b, 0, hg))),
            pl.BlockSpec((1, s, width), idx(lambda b, hg, i: (b, 0, hg))),
            pl.BlockSpec((1, s, LANES), idx(lambda b, hg, i: (b, 0, 0))),
        ],
        out_specs=pl.BlockSpec((1, blk, width), idx(lambda b, hg, i: (b, i, hg))),
    )
    return pl.pallas_call(
        functools.partial(_fox_attn_kernel, blk=blk, nblk=nblk, npairs=npairs),
        grid_spec=grid_spec,
        out_shape=jax.ShapeDtypeStruct((bsz, s, d), BF16),
        compiler_params=_compiler_params(("arbitrary", "arbitrary", "arbitrary")),
        name="fox_attention",
    )(flat(rblk), flat(qnblk), knflat, q, qaug, k, v, kaug)


def _proj_mlp_kernel(a_ref, wo_ref, x_ref, mod_ref, g1_ref, b1_ref, w1_ref, w2_ref, g2_ref, b2_ref, o_ref,
                     *, ff_chunk):
    y = jnp.dot(a_ref[0], wo_ref[...], preferred_element_type=F32)
    x1 = _layer_norm(RESIDUAL_ALPHA * x_ref[0] + (1.0 + mod_ref[0, 2:3, :]) * y, g1_ref[...], b1_ref[...])
    u = (x1 * (1.0 + mod_ref[0, 4:5, :]) + mod_ref[0, 3:4, :]).astype(BF16)
    y = None
    for c in range(D_FF // ff_chunk):
        h = jnp.dot(u, w1_ref[:, c * ff_chunk:(c + 1) * ff_chunk], preferred_element_type=F32)
        h = jnp.square(jnp.maximum(h, 0.0)).astype(BF16)
        part = jnp.dot(h, w2_ref[c * ff_chunk:(c + 1) * ff_chunk, :], preferred_element_type=F32)
        y = part if y is None else y + part
    r = RESIDUAL_ALPHA * x1 + (1.0 + mod_ref[0, 5:6, :]) * y
    o_ref[0] = _layer_norm(r, g2_ref[...], b2_ref[...])


def _proj_mlp(a, w_o, x, mod, ln1_g, ln1_b, w1, w2, ln2_g, ln2_b):
    bsz, s, d = x.shape
    kdim = a.shape[-1]
    tm = min(ROW_TILE, s)
    rows = lambda width: pl.BlockSpec((1, tm, width), lambda b, i: (b, i, 0))
    resident = lambda shape: pl.BlockSpec(shape, lambda b, i: (0,) * len(shape), pipeline_mode=pl.Buffered(1))
    vec = pl.BlockSpec((1, d), lambda b, i: (0, 0))
    return pl.pallas_call(
        functools.partial(_proj_mlp_kernel, ff_chunk=1024),
        grid=(bsz, s // tm),
        in_specs=[rows(kdim), resident((kdim, d)), rows(d), pl.BlockSpec((1, 6, d), lambda b, i: (b, 0, 0)),
                  vec, vec, resident((d, D_FF)), resident((D_FF, d)), vec, vec],
        out_specs=rows(d),
        out_shape=jax.ShapeDtypeStruct((bsz, s, d), F32),
        compiler_params=_compiler_params(("arbitrary", "arbitrary")),
        name="proj_mlp",
    )(a, w_o.astype(BF16), x, mod, ln1_g.reshape(1, d), ln1_b.reshape(1, d), w1.astype(BF16), w2.astype(BF16),
      ln2_g.reshape(1, d), ln2_b.reshape(1, d))


def _ssd_inproj_kernel(x_ref, mod_ref, w_ref, convw_ref, convb_ref, z_ref, xs_ref, bc_ref, dt_ref, ext_ref,
                       *, tm, chunk):
    halo = SUBLANES

    @pl.when(pl.program_id(1) == 0)
    def _():
        ext_ref[0:halo, :] = jnp.zeros((halo, SSM_CONV_DIM), F32)

    u = (x_ref[0] * (1.0 + mod_ref[0, 1:2, :]) + mod_ref[0, 0:1, :]).astype(BF16)
    for c in range(SSM_D_INNER // chunk):
        z_ref[0, :, c * chunk:(c + 1) * chunk] = jnp.dot(
            u, w_ref[:, c * chunk:(c + 1) * chunk], preferred_element_type=F32)
    for c in range(SSM_CONV_DIM // chunk):
        cols = slice(c * chunk, (c + 1) * chunk)
        ext_ref[halo:halo + tm, cols] = jnp.dot(
            u, w_ref[:, SSM_D_INNER + c * chunk:SSM_D_INNER + (c + 1) * chunk], preferred_element_type=F32)
        ext = ext_ref[:, cols]
        ext1 = pltpu.roll(ext, 1, axis=0)
        near = ext * convw_ref[3:4, cols] + ext1 * convw_ref[2:3, cols]
        far = ext * convw_ref[1:2, cols] + ext1 * convw_ref[0:1, cols]
        conv = (near + pltpu.roll(far, 2, axis=0))[halo:halo + tm] + convb_ref[:, cols]
        ext_ref[0:halo, cols] = ext_ref[tm:tm + halo, cols]
        act = _silu(conv)
        if (c + 1) * chunk <= SSM_D_INNER:
            xs_ref[0, :, cols] = act
        else:
            bc_ref[0, :, c * chunk - SSM_D_INNER:(c + 1) * chunk - SSM_D_INNER] = act.astype(BF16)
    off = SSM_D_INNER + SSM_CONV_DIM
    dt = jnp.dot(u, w_ref[:, off:off + SSM_HEADS], preferred_element_type=F32)
    dt_ref[0] = jnp.concatenate([dt, jnp.zeros((tm, LANES - SSM_HEADS), F32)], axis=1)


def _ssd_inproj(x, mod, w_in, conv_w, conv_b):
    bsz, s, d = x.shape
    tm = min(ROW_TILE, s)
    w = w_in.astype(BF16)
    rows = lambda width: pl.BlockSpec((1, tm, width), lambda b, i: (b, i, 0))
    const = lambda shape: pl.BlockSpec(shape, lambda b, i: (0,) * len(shape))
    return pl.pallas_call(
        functools.partial(_ssd_inproj_kernel, tm=tm, chunk=1024),
        grid=(bsz, s // tm),
        in_specs=[rows(d), pl.BlockSpec((1, 6, d), lambda b, i: (b, 0, 0)),
                  pl.BlockSpec(w.shape, lambda b, i: (0, 0), pipeline_mode=pl.Buffered(1)),
                  const((SSM_CONV, SSM_CONV_DIM)), const((1, SSM_CONV_DIM))],
        out_specs=[rows(SSM_D_INNER), rows(SSM_D_INNER), rows(2 * SSM_BC_WIDTH), rows(LANES)],
        out_shape=[jax.ShapeDtypeStruct((bsz, s, SSM_D_INNER), F32),
                   jax.ShapeDtypeStruct((bsz, s, SSM_D_INNER), F32),
                   jax.ShapeDtypeStruct((bsz, s, 2 * SSM_BC_WIDTH), BF16),
                   jax.ShapeDtypeStruct((bsz, s, LANES), F32)],
        scratch_shapes=[pltpu.VMEM((SUBLANES + tm, SSM_CONV_DIM), F32)],
        compiler_params=_compiler_params(("arbitrary", "arbitrary")),
        name="ssd_inproj",
    )(x, mod, w, conv_w.astype(F32), conv_b.reshape(1, -1).astype(F32))


def _ssd_core_kernel(xs_ref, bc_ref, z_ref, dt_ref, dtb_ref, alog_ref, dskip_ref,
                     normw_ref, tri_ref, xp_ref,
                     o_ref, state_ref, *, nchunks):
    ln = SSM_CHUNK

    @pl.when(pl.program_id(1) == 0)
    def _():
        state_ref[...] = jnp.zeros_like(state_ref)

    for chunk in range(nchunks):
        _ssd_chunk(slice(chunk * ln, (chunk + 1) * ln), xs_ref, bc_ref, z_ref, dt_ref, dtb_ref, alog_ref,
                   dskip_ref, normw_ref, tri_ref, xp_ref, o_ref, state_ref)


def _ssd_chunk(rows, xs_ref, bc_ref, z_ref, dt_ref, dtb_ref, alog_ref, dskip_ref, normw_ref, tri_ref,
               xp_ref, o_ref, state_ref):
    ln = SSM_CHUNK
    xs = xs_ref[0, rows, :]

    dt = _softplus(dt_ref[0, rows, :] + dtb_ref[...])
    dt_rows = dt.T
    dta = dt * (-jnp.exp(alog_ref[...]))
    acs = jnp.dot(tri_ref[...], jnp.concatenate(_split3(dta), axis=0), preferred_element_type=F32)
    acs_last = acs[ln - 1:ln, :]
    dtdte = dt * jnp.exp(acs_last - acs)
    acs_rows = acs.T

    def expand(v):
        return jnp.dot(jnp.concatenate(_split3(v), axis=1), xp_ref[...], preferred_element_type=F32)

    xb = xs.astype(BF16)
    xdte = (xs * expand(dtdte)).astype(BF16)
    eacs_x = expand(jnp.exp(acs))

    rr = lax.broadcasted_iota(jnp.int32, (ln, ln), 0)
    cc = lax.broadcasted_iota(jnp.int32, (ln, ln), 1)
    causal = rr >= cc
    lane_grp = lax.broadcasted_iota(jnp.int32, (1, SSM_GROUP_WIDTH), 1)
    zero = jnp.zeros((), BF16)
    nt = (((1,), (1,)), ((), ()))
    tn = (((0,), (0,)), ((), ()))

    for g in range(SSM_GROUPS):
        gs = slice(g * SSM_GROUP_WIDTH, (g + 1) * SSM_GROUP_WIDTH)
        bg = bc_ref[0, rows, g * SSM_STATE:(g + 1) * SSM_STATE]
        cg = bc_ref[0, rows, SSM_BC_WIDTH + g * SSM_STATE:SSM_BC_WIDTH + (g + 1) * SSM_STATE]
        cb = lax.dot_general(cg, bg, nt, preferred_element_type=F32)
        xg = xb[:, gs]
        y = None
        for r in range(SSM_HEADS_PER_GROUP):
            h = g * SSM_HEADS_PER_GROUP + r
            seg = acs[:, h:h + 1] - acs_rows[h:h + 1, :]
            decay = jnp.exp(jnp.where(causal, seg, -jnp.inf))
            mix = (cb * decay * dt_rows[h:h + 1, :]).astype(BF16)
            own = (lane_grp >= r * SSM_HEAD_DIM) & (lane_grp < (r + 1) * SSM_HEAD_DIM)
            part = jnp.dot(mix, jnp.where(own, xg, zero), preferred_element_type=F32)
            y = part if y is None else y + part
        prev = state_ref[g]
        y = y + jnp.dot(cg, prev.astype(BF16), preferred_element_type=F32) * eacs_x[:, gs]
        y = y + xs[:, gs] * dskip_ref[:, gs]
        new_states = lax.dot_general(bg, xdte[:, gs], tn, preferred_element_type=F32)
        state_ref[g] = prev * eacs_x[ln - 1:ln, gs] + new_states
        zg = z_ref[0, rows, gs]
        gated = y * _silu(zg)
        ms = jnp.mean(gated * gated, axis=-1, keepdims=True)
        o_ref[0, rows, gs] = (gated * lax.rsqrt(ms + RMS_EPS) * normw_ref[:, gs]).astype(o_ref.dtype)


def _ssd_constants():
    ln = SSM_CHUNK
    t = jnp.arange(ln)
    tri = (t[:, None] >= t[None, :]).astype(BF16)
    tri3 = jnp.concatenate([tri] * SPLIT, axis=1)
    head = jnp.arange(SPLIT * LANES) % LANES
    col = jnp.arange(SSM_D_INNER)
    xp = ((head < SSM_HEADS)[:, None] & (head[:, None] == (col // SSM_HEAD_DIM)[None, :])).astype(BF16)
    return tri3, xp


def _ssd_core(z, xs, bc, dt, dt_bias, a_log, d_skip, norm_w):
    bsz, s, _ = z.shape
    nchunks = SSD_CHUNKS_PER_STEP
    ln = nchunks * SSM_CHUNK
    pad_heads = lambda v: jnp.pad(v.astype(F32), (0, LANES - SSM_HEADS)).reshape(1, LANES)
    consts = _ssd_constants()
    const = lambda a: pl.BlockSpec(a.shape, lambda b, i: (0,) * a.ndim)
    rows = lambda width: pl.BlockSpec((1, ln, width), lambda b, i: (b, i, 0))
    small = [pad_heads(dt_bias), pad_heads(a_log),
             jnp.repeat(d_skip.astype(F32), SSM_HEAD_DIM).reshape(1, -1), norm_w.reshape(1, -1).astype(F32)]
    return pl.pallas_call(
        functools.partial(_ssd_core_kernel, nchunks=nchunks),
        grid=(bsz, s // ln),
        in_specs=[rows(SSM_D_INNER), rows(2 * SSM_BC_WIDTH), rows(SSM_D_INNER), rows(LANES)]
        + [const(a) for a in small] + [const(a) for a in consts],
        out_specs=rows(SSM_D_INNER),
        out_shape=jax.ShapeDtypeStruct((bsz, s, SSM_D_INNER), BF16),
        scratch_shapes=[pltpu.VMEM((SSM_GROUPS, SSM_STATE, SSM_GROUP_WIDTH), F32)],
        compiler_params=_compiler_params(("arbitrary", "arbitrary")),
        name="ssd_core",
    )(xs, bc, z, dt, *small, *consts)


def kernel(x, c, ada_w, ada_b, ln_mix_g, ln_mix_b, ln_mlp_g, ln_mlp_b, mlp_w1, mlp_w2, fox_w_in, fox_b_f, fox_w_o, ssm_w_in, ssm_conv_w, ssm_conv_b, ssm_dt_bias, ssm_a_log, ssm_d, ssm_norm_w, ssm_w_out):
    mod = _ada_modulation(c, ada_w, ada_b)

    q, k, v, qaug, kaug, rblk, qnblk, kntile = _fox_inproj(x, mod[0], fox_w_in[0], fox_b_f[0])
    attn = _fox_attention(q, k, v, qaug, kaug, rblk, qnblk, kntile)
    x = _proj_mlp(attn, fox_w_o[0], x, mod[0], ln_mix_g[0], ln_mix_b[0], mlp_w1[0], mlp_w2[0],
                  ln_mlp_g[0], ln_mlp_b[0])

    z, xs, bc, dt = _ssd_inproj(x, mod[1], ssm_w_in[0], ssm_conv_w[0], ssm_conv_b[0])
    y = _ssd_core(z, xs, bc, dt, ssm_dt_bias[0], ssm_a_log[0], ssm_d[0], ssm_norm_w[0])
    x = _proj_mlp(y, ssm_w_out[0], x, mod[1], ln_mix_g[1], ln_mix_b[1], mlp_w1[1], mlp_w2[1],
                  ln_mlp_g[1], ln_mlp_b[1])
    return x
```

```python
import functools

import jax
import jax.numpy as jnp
from jax import lax
from jax.experimental import pallas as pl
from jax.experimental.pallas import tpu as pltpu

F32 = jnp.float32
BF16 = jnp.bfloat16

D_MODEL = 1024
DEPTH = 2
FOX_HEADS = 16
FOX_HEAD_DIM = D_MODEL // FOX_HEADS
SSM_D_INNER = 2 * D_MODEL
SSM_HEAD_DIM = 64
SSM_HEADS = SSM_D_INNER // SSM_HEAD_DIM
SSM_GROUPS = 8
SSM_HEADS_PER_GROUP = SSM_HEADS // SSM_GROUPS
SSM_STATE = 128
SSM_CONV = 4
SSM_CHUNK = 128
SSM_GROUP_WIDTH = SSM_D_INNER // SSM_GROUPS
SSM_BC_WIDTH = SSM_GROUPS * SSM_STATE
SSM_CONV_DIM = SSM_D_INNER + 2 * SSM_BC_WIDTH
D_FF = 4 * D_MODEL
LN_EPS = 1e-5
RMS_EPS = 1e-5
RESIDUAL_ALPHA = (2.0 * DEPTH) ** 0.25

LANES = 128
SUBLANES = 8
VMEM_LIMIT_BYTES = 56 * 1024 * 1024

ATTN_BLOCK = 512
ATTN_PAIRS_PER_STEP = 2
PREFIX_ROWS = 128
ROW_TILE = 512
SSD_CHUNKS_PER_STEP = 4
AUG_PER_HEAD = 6
SPLIT = 3
NORM_SLACK = 1.01
SKIP_LOG_MARGIN = 32.0
FAST_MAX_GAP = 60.0


def _split3(v):
    hi = v.astype(BF16)
    r1 = v - hi.astype(F32)
    mid = r1.astype(BF16)
    lo = (r1 - mid.astype(F32)).astype(BF16)
    return hi, mid, lo


def _softplus(y):
    return jnp.maximum(y, 0.0) + jnp.log1p(jnp.exp(-jnp.abs(y)))


def _silu(y):
    h = 0.5 * y
    return h + h * jnp.tanh(h)


def _layer_norm(r, g, b):
    mu = jnp.mean(r, axis=-1, keepdims=True)
    cen = r - mu
    var = jnp.mean(cen * cen, axis=-1, keepdims=True)
    return cen * lax.rsqrt(var + LN_EPS) * g + b


def _compiler_params(semantics):
    return pltpu.CompilerParams(dimension_semantics=semantics, vmem_limit_bytes=VMEM_LIMIT_BYTES)


def _ada_kernel(c_ref, w_ref, b_ref, o_ref):
    c = c_ref[...]
    cond = _silu(c)
    o_ref[0] = jnp.dot(cond, w_ref[0], precision=lax.Precision.HIGHEST,
                       preferred_element_type=F32) + b_ref[0]


def _ada_modulation(c, ada_w, ada_b):
    depth, d, n = ada_w.shape
    bsz = c.shape[0]
    tn = 1536
    out = pl.pallas_call(
        _ada_kernel,
        grid=(depth, n // tn),
        in_specs=[
            pl.BlockSpec((bsz, d), lambda i, j: (0, 0)),
            pl.BlockSpec((1, d, tn), lambda i, j: (i, 0, j)),
            pl.BlockSpec((1, 1, tn), lambda i, j: (i, 0, j)),
        ],
        out_specs=pl.BlockSpec((1, bsz, tn), lambda i, j: (i, 0, j)),
        out_shape=jax.ShapeDtypeStruct((depth, bsz, n), F32),
        compiler_params=_compiler_params(("arbitrary", "arbitrary")),
        name="ada_modulation",
    )(c, ada_w, ada_b.reshape(depth, 1, n))
    return out.reshape(depth, bsz, 6, d)


def _fox_inproj_kernel(x_ref, mod_ref, w_ref, bf_ref, tri_ref, eq_ref, ek_ref, oq_ref, ok_ref, hs_ref,
                       q_ref, k_ref, v_ref, qaug_ref, kaug_ref, rblk_ref, rmid_ref, qn_ref, kn_ref,
                       carry_ref, kmax_ref, *, tm, gb):
    @pl.when(pl.program_id(1) == 0)
    def _():
        carry_ref[...] = jnp.zeros_like(carry_ref)
        kmax_ref[...] = jnp.zeros_like(kmax_ref)

    x = x_ref[0]
    u = (x * (1.0 + mod_ref[0, 1:2, :]) + mod_ref[0, 0:1, :]).astype(BF16)
    d = D_MODEL
    qb = (jnp.dot(u, w_ref[:, 0:d], preferred_element_type=F32) * (FOX_HEAD_DIM ** -0.5)).astype(BF16)
    kb = jnp.dot(u, w_ref[:, d:2 * d], preferred_element_type=F32).astype(BF16)
    q_ref[0] = qb
    k_ref[0] = kb

    def head_norms(t):
        t = t.astype(F32)
        return jnp.sqrt(jnp.dot((t * t).astype(BF16), hs_ref[...], preferred_element_type=F32)) * NORM_SLACK

    qnorm = head_norms(qb)
    kmax_ref[...] = jnp.maximum(kmax_ref[...], jnp.max(head_norms(kb), axis=0, keepdims=True))
    kn_ref[0, 0] = kmax_ref[...]
    v_ref[0] = jnp.dot(u, w_ref[:, 2 * d:3 * d], preferred_element_type=F32).astype(BF16)
    f = jnp.dot(u, w_ref[:, 3 * d:3 * d + FOX_HEADS], preferred_element_type=F32)
    f = jnp.concatenate([f, jnp.zeros((tm, LANES - FOX_HEADS), F32)], axis=1) + bf_ref[...]
    logf = jnp.minimum(f, 0.0) - jnp.log1p(jnp.exp(-jnp.abs(f)))

    for g in range(tm // gb):
        pieces, run = [], jnp.zeros((1, LANES), F32)
        for k in range(gb // PREFIX_ROWS):
            lf = logf[g * gb + k * PREFIX_ROWS:g * gb + (k + 1) * PREFIX_ROWS]
            part = jnp.dot(tri_ref[...], jnp.concatenate(_split3(lf), axis=0), preferred_element_type=F32) + run
            pieces.append(part)
            run = part[PREFIX_ROWS - 1:PREFIX_ROWS, :]
        rel = jnp.concatenate(pieces, axis=0)
        rcat = jnp.concatenate(_split3(rel), axis=1)
        qaug_ref[0, g * gb:(g + 1) * gb, :] = (
            jnp.dot(rcat, eq_ref[...], preferred_element_type=F32) + oq_ref[...]).astype(BF16)
        kaug_ref[0, g * gb:(g + 1) * gb, :] = (
            jnp.dot(rcat, ek_ref[...], preferred_element_type=F32) + ok_ref[...]).astype(BF16)
        rblk_ref[0, 0, g:g + 1, :] = carry_ref[...]
        rmid_ref[0, 0, g:g + 1, :] = carry_ref[...] + rel[gb // 2 - 1:gb // 2, :]
        qn_ref[0, 0, g:g + 1, :] = jnp.max(qnorm[g * gb:(g + 1) * gb], axis=0, keepdims=True)
        carry_ref[...] = carry_ref[...] + rel[gb - 1:gb, :]


def _fox_aug_constants():
    lane = jnp.arange(LANES)
    row = jnp.arange(SPLIT * LANES)
    piece, head = row // LANES, row % LANES
    valid = head < FOX_HEADS
    eq = (valid[:, None] & (lane[None, :] == (AUG_PER_HEAD * head + piece)[:, None])).astype(BF16)
    ek = -(valid[:, None] & (lane[None, :] == (AUG_PER_HEAD * head + SPLIT + piece)[:, None])).astype(BF16)
    used = lane < AUG_PER_HEAD * FOX_HEADS
    oq = (used & (lane % AUG_PER_HEAD >= SPLIT)).astype(F32)[None, :]
    ok = (used & (lane % AUG_PER_HEAD < SPLIT)).astype(F32)[None, :]
    t = jnp.arange(PREFIX_ROWS)
    tri = (t[:, None] >= t[None, :]).astype(BF16)
    tri3 = jnp.concatenate([tri] * SPLIT, axis=1)
    hs = (jnp.arange(D_MODEL)[:, None] // FOX_HEAD_DIM == lane[None, :]).astype(BF16)
    return tri3, eq, ek, oq, ok, hs


def _fox_inproj(x, mod, w_in, b_f):
    bsz, s, d = x.shape
    tm, gb = min(ROW_TILE, s), ATTN_BLOCK
    w = w_in.astype(BF16)
    bf = jnp.pad(b_f, (0, LANES - FOX_HEADS)).reshape(1, LANES).astype(F32)
    tri3, eq, ek, oq, ok, hs = _fox_aug_constants()
    const = lambda shape: pl.BlockSpec(shape, lambda b, i: (0,) * len(shape))
    rows = lambda width: pl.BlockSpec((1, tm, width), lambda b, i: (b, i, 0))
    per_block = pl.BlockSpec((1, 1, tm // gb, LANES), lambda b, i: (b, i, 0, 0))
    return pl.pallas_call(
        functools.partial(_fox_inproj_kernel, tm=tm, gb=gb),
        grid=(bsz, s // tm),
        in_specs=[rows(d), pl.BlockSpec((1, 6, d), lambda b, i: (b, 0, 0)), const(w.shape),
                  const(bf.shape), const(tri3.shape), const(eq.shape), const(ek.shape),
                  const(oq.shape), const(ok.shape), const(hs.shape)],
        out_specs=[rows(d), rows(d), rows(d), rows(LANES), rows(LANES), per_block, per_block, per_block,
                   pl.BlockSpec((1, 1, 1, LANES), lambda b, i: (b, i, 0, 0))],
        out_shape=[jax.ShapeDtypeStruct((bsz, s, d), BF16)] * 3
        + [jax.ShapeDtypeStruct((bsz, s, LANES), BF16)] * 2
        + [jax.ShapeDtypeStruct((bsz, s // tm, tm // gb, LANES), F32)] * 3
        + [jax.ShapeDtypeStruct((bsz, s // tm, 1, LANES), F32)],
        scratch_shapes=[pltpu.VMEM((1, LANES), F32), pltpu.VMEM((1, LANES), F32)],
        compiler_params=_compiler_params(("arbitrary", "arbitrary")),
        name="fox_inproj",
    )(x, mod, w, bf, tri3, eq, ek, oq, ok, hs)


def _fox_attn_kernel(r_ref, rmid_ref, qn_ref, kn_ref, q_ref, qaug_ref, k_ref, v_ref, kaug_ref, o_ref,
                     *, blk, nblk, npairs):
    b, hg, i = pl.program_id(0), pl.program_id(1), pl.program_id(2)
    nheads = 2 * npairs
    lane = lax.broadcasted_iota(jnp.int32, (1, LANES), 1)
    qa = qaug_ref[0]
    zero = jnp.zeros((), BF16)
    nt = (((1,), (1,)), ((), ()))
    slab = lambda pair: slice(pair * LANES, (pair + 1) * LANES)

    qcat, own, rbase, ubound, first, far_dead = [], [], [], [], [], []
    for hh in range(nheads):
        h = nheads * hg + hh
        half = hh % 2
        own.append((lane >= half * FOX_HEAD_DIM) & (lane < (half + 1) * FOX_HEAD_DIM))
        aug = (lane >= AUG_PER_HEAD * h) & (lane < AUG_PER_HEAD * (h + 1))
        qcat.append(jnp.concatenate([jnp.where(own[hh], q_ref[0, :, slab(hh // 2)], zero),
                                     jnp.where(aug, qa, zero)], axis=1))
        base = (b * FOX_HEADS + h) * nblk
        rbase.append(base)
        ub = qn_ref[base + i] * kn_ref[b * FOX_HEADS + h]
        ubound.append(ub)
        thresh = -(SKIP_LOG_MARGIN + 2.0 * ub)
        r_i = r_ref[base + i]
        first.append(lax.while_loop(
            lambda j, base=base, r_i=r_i, thresh=thresh: (j > 0) & (r_i - r_ref[base + j] >= thresh),
            lambda j: j - 1, i))
        far_dead.append(rmid_ref[base + i] - rmid_ref[base + jnp.maximum(i - 1, 0)] < thresh)
    j_first = functools.reduce(jnp.minimum, first)
    fast = 2.0 * functools.reduce(jnp.maximum, ubound) <= FAST_MAX_GAP
    skip_far_quarter = functools.reduce(jnp.logical_and, far_dead)

    def block_shift(hh, j):
        return r_ref[rbase[hh] + i] - r_ref[rbase[hh] + j]

    def causal_mask(z):
        rr = lax.broadcasted_iota(jnp.int32, z.shape, 0)
        cc = lax.broadcasted_iota(jnp.int32, z.shape, 1)
        return jnp.where(rr >= cc, z, -jnp.inf)

    def key_rows(j):
        return pl.ds(pl.multiple_of(j * blk, blk), blk)

    def key_operand(j, pair):
        return jnp.concatenate([k_ref[0, key_rows(j), slab(pair)], kaug_ref[0, key_rows(j), :]], axis=1)

    @pl.when(fast)
    def _():
        ones_col = [jnp.broadcast_to(jnp.where(lane == half, 1.0, 0.0), (blk, LANES)).astype(BF16)
                    for half in range(2)]

        def tile(j, kind):
            hb = blk // 2
            outs = []
            for pair in range(npairs):
                kcat = key_operand(j, pair)
                vs = v_ref[0, key_rows(j), slab(pair)]
                out = None
                for half in range(2):
                    hh = 2 * pair + half
                    shift = block_shift(hh, j) - ubound[hh]
                    vaug = jnp.concatenate([jnp.where(own[hh], vs, zero), ones_col[half]], axis=1)
                    if kind == "diagonal":
                        zl = lax.dot_general(qcat[hh], kcat[:hb], nt, preferred_element_type=F32) + shift
                        zr = lax.dot_general(qcat[hh][hb:], kcat[hb:], nt, preferred_element_type=F32) + shift
                        part = jnp.dot(jnp.exp(causal_mask(zl)).astype(BF16), vaug[:hb],
                                       preferred_element_type=F32)
                        lower = jnp.dot(jnp.exp(causal_mask(zr)).astype(BF16), vaug[hb:],
                                        preferred_element_type=F32)
                        part = jnp.concatenate([part[:hb], part[hb:] + lower], axis=0)
                    elif kind == "near":
                        zt = lax.dot_general(qcat[hh][:hb], kcat[:hb], nt, preferred_element_type=F32) + shift
                        zr = lax.dot_general(qcat[hh], kcat[hb:], nt, preferred_element_type=F32) + shift
                        part = jnp.dot(jnp.exp(zr).astype(BF16), vaug[hb:], preferred_element_type=F32)
                        upper = jnp.dot(jnp.exp(zt).astype(BF16), vaug[:hb], preferred_element_type=F32)
                        part = jnp.concatenate([part[:hb] + upper, part[hb:]], axis=0)
                    else:
                        z = lax.dot_general(qcat[hh], kcat, nt, preferred_element_type=F32) + shift
                        part = jnp.dot(jnp.exp(z).astype(BF16), vaug, preferred_element_type=F32)
                    out = part if out is None else out + part
                outs.append(out)
            return tuple(outs)

        def add(accs, parts):
            return tuple(a + p for a, p in zip(accs, parts))

        def finish(accs):
            for pair, acc in enumerate(accs):
                l = jnp.where(own[0], acc[:, LANES:LANES + 1], acc[:, LANES + 1:LANES + 2])
                o_ref[0, :, slab(pair)] = (acc[:, :LANES] / l).astype(o_ref.dtype)

        def with_off_diagonal(nearest_kind):
            init = tuple(jnp.zeros((blk, 2 * LANES), F32) for _ in range(npairs))
            accs = lax.fori_loop(j_first, i - 1, lambda j, a: add(a, tile(j, "full")), init)
            finish(add(add(accs, tile(i - 1, nearest_kind)), tile(i, "diagonal")))

        @pl.when((j_first < i) & skip_far_quarter)
        def _():
            with_off_diagonal("near")

        @pl.when((j_first < i) & jnp.logical_not(skip_far_quarter))
        def _():
            with_off_diagonal("full")

        @pl.when(j_first >= i)
        def _():
            finish(tile(i, "diagonal"))

    @pl.when(jnp.logical_not(fast))
    def _():
        def tile(j, carry, diagonal):
            out = []
            for hh in range(nheads):
                m, l, acc = carry[hh]
                z = lax.dot_general(qcat[hh], key_operand(j, hh // 2), nt, preferred_element_type=F32)
                dij = block_shift(hh, j)
                if diagonal:
                    z = causal_mask(z)
                m_new = jnp.maximum(m, jnp.max(z, axis=1, keepdims=True) + dij)
                p = jnp.exp(z - (m_new - dij))
                alpha = jnp.exp(m - m_new)
                l = alpha * l + jnp.sum(p, axis=1, keepdims=True)
                vm = jnp.where(own[hh], v_ref[0, key_rows(j), slab(hh // 2)], zero)
                acc = alpha * acc + jnp.dot(p.astype(BF16), vm, preferred_element_type=F32)
                out.append((m_new, l, acc))
            return tuple(out)

        init = tuple((jnp.full((blk, 1), -jnp.inf, F32), jnp.zeros((blk, 1), F32),
                      jnp.zeros((blk, LANES), F32)) for _ in range(nheads))
        carry = lax.fori_loop(j_first, i, lambda j, c: tile(j, c, False), init)
        carry = tile(i, carry, True)
        for pair in range(npairs):
            (_, l0, a0), (_, l1, a1) = carry[2 * pair], carry[2 * pair + 1]
            o_ref[0, :, slab(pair)] = (a0 / l0 + a1 / l1).astype(o_ref.dtype)


def _fox_attention(q, k, v, qaug, kaug, rblk, rmid, qnblk, kntile):
    bsz, s, d = q.shape
    blk = ATTN_BLOCK
    nblk = s // blk
    npairs = ATTN_PAIRS_PER_STEP
    width = npairs * LANES
    flat = lambda a: a.reshape(bsz, nblk, LANES)[:, :, :FOX_HEADS].transpose(0, 2, 1).reshape(-1)
    knflat = kntile[:, -1, 0, :FOX_HEADS].reshape(-1)
    idx = lambda f: (lambda b, hg, i, r, rm, qn, kn: f(b, hg, i))
    grid_spec = pltpu.PrefetchScalarGridSpec(
        num_scalar_prefetch=4,
        grid=(bsz, FOX_HEADS // (2 * npairs), nblk),
        in_specs=[
            pl.BlockSpec((1, blk, width), idx(lambda b, hg, i: (b, i, hg))),
            pl.BlockSpec((1, blk, LANES), idx(lambda b, hg, i: (b, i, 0))),
            pl.BlockSpec((1, s, width), idx(lambda b, hg, i: (b, 0, hg))),
            pl.BlockSpec((1, s, width), idx(lambda b, hg, i: (b, 0, hg))),
            pl.BlockSpec((1, s, LANES), idx(lambda b, hg, i: (b, 0, 0))),
        ],
        out_specs=pl.BlockSpec((1, blk, width), idx(lambda b, hg, i: (b, i, hg))),
    )
    return pl.pallas_call(
        functools.partial(_fox_attn_kernel, blk=blk, nblk=nblk, npairs=npairs),
        grid_spec=grid_spec,
        out_shape=jax.ShapeDtypeStruct((bsz, s, d), BF16),
        compiler_params=_compiler_params(("arbitrary", "arbitrary", "arbitrary")),
        name="fox_attention",
    )(flat(rblk), flat(rmid), flat(qnblk), knflat, q, qaug, k, v, kaug)


def _proj_mlp_kernel(a_ref, wo_ref, x_ref, mod_ref, g1_ref, b1_ref, w1_ref, w2_ref, g2_ref, b2_ref, o_ref,
                     *, ff_chunk):
    y = jnp.dot(a_ref[0], wo_ref[...], preferred_element_type=F32)
    x1 = _layer_norm(RESIDUAL_ALPHA * x_ref[0] + (1.0 + mod_ref[0, 2:3, :]) * y, g1_ref[...], b1_ref[...])
    u = (x1 * (1.0 + mod_ref[0, 4:5, :]) + mod_ref[0, 3:4, :]).astype(BF16)
    y = None
    for c in range(D_FF // ff_chunk):
        h = jnp.dot(u, w1_ref[:, c * ff_chunk:(c + 1) * ff_chunk], preferred_element_type=F32)
        h = jnp.square(jnp.maximum(h, 0.0)).astype(BF16)
        part = jnp.dot(h, w2_ref[c * ff_chunk:(c + 1) * ff_chunk, :], preferred_element_type=F32)
        y = part if y is None else y + part
    r = RESIDUAL_ALPHA * x1 + (1.0 + mod_ref[0, 5:6, :]) * y
    o_ref[0] = _layer_norm(r, g2_ref[...], b2_ref[...])


def _proj_mlp(a, w_o, x, mod, ln1_g, ln1_b, w1, w2, ln2_g, ln2_b):
    bsz, s, d = x.shape
    kdim = a.shape[-1]
    tm = min(ROW_TILE, s)
    rows = lambda width: pl.BlockSpec((1, tm, width), lambda b, i: (b, i, 0))
    resident = lambda shape: pl.BlockSpec(shape, lambda b, i: (0,) * len(shape), pipeline_mode=pl.Buffered(1))
    vec = pl.BlockSpec((1, d), lambda b, i: (0, 0))
    return pl.pallas_call(
        functools.partial(_proj_mlp_kernel, ff_chunk=1024),
        grid=(bsz, s // tm),
        in_specs=[rows(kdim), resident((kdim, d)), rows(d), pl.BlockSpec((1, 6, d), lambda b, i: (b, 0, 0)),
                  vec, vec, resident((d, D_FF)), resident((D_FF, d)), vec, vec],
        out_specs=rows(d),
        out_shape=jax.ShapeDtypeStruct((bsz, s, d), F32),
        compiler_params=_compiler_params(("arbitrary", "arbitrary")),
        name="proj_mlp",
    )(a, w_o.astype(BF16), x, mod, ln1_g.reshape(1, d), ln1_b.reshape(1, d), w1.astype(BF16), w2.astype(BF16),
      ln2_g.reshape(1, d), ln2_b.reshape(1, d))


def _ssd_inproj_kernel(x_ref, mod_ref, w_ref, convw_ref, convb_ref, z_ref, xs_ref, bc_ref, dt_ref, ext_ref,
                       *, tm, chunk):
    halo = SUBLANES

    @pl.when(pl.program_id(1) == 0)
    def _():
        ext_ref[0:halo, :] = jnp.zeros((halo, SSM_CONV_DIM), F32)

    u = (x_ref[0] * (1.0 + mod_ref[0, 1:2, :]) + mod_ref[0, 0:1, :]).astype(BF16)
    for c in range(SSM_D_INNER // chunk):
        z_ref[0, :, c * chunk:(c + 1) * chunk] = jnp.dot(
            u, w_ref[:, c * chunk:(c + 1) * chunk], preferred_element_type=F32)
    for c in range(SSM_CONV_DIM // chunk):
        cols = slice(c * chunk, (c + 1) * chunk)
        ext_ref[halo:halo + tm, cols] = jnp.dot(
            u, w_ref[:, SSM_D_INNER + c * chunk:SSM_D_INNER + (c + 1) * chunk], preferred_element_type=F32)
        ext = ext_ref[:, cols]
        ext1 = pltpu.roll(ext, 1, axis=0)
        near = ext * convw_ref[3:4, cols] + ext1 * convw_ref[2:3, cols]
        far = ext * convw_ref[1:2, cols] + ext1 * convw_ref[0:1, cols]
        conv = (near + pltpu.roll(far, 2, axis=0))[halo:halo + tm] + convb_ref[:, cols]
        ext_ref[0:halo, cols] = ext_ref[tm:tm + halo, cols]
        act = _silu(conv)
        if (c + 1) * chunk <= SSM_D_INNER:
            xs_ref[0, :, cols] = act
        else:
            bc_ref[0, :, c * chunk - SSM_D_INNER:(c + 1) * chunk - SSM_D_INNER] = act.astype(BF16)
    off = SSM_D_INNER + SSM_CONV_DIM
    dt = jnp.dot(u, w_ref[:, off:off + SSM_HEADS], preferred_element_type=F32)
    dt_ref[0] = jnp.concatenate([dt, jnp.zeros((tm, LANES - SSM_HEADS), F32)], axis=1)


def _ssd_inproj(x, mod, w_in, conv_w, conv_b):
    bsz, s, d = x.shape
    tm = min(ROW_TILE, s)
    w = w_in.astype(BF16)
    rows = lambda width: pl.BlockSpec((1, tm, width), lambda b, i: (b, i, 0))
    const = lambda shape: pl.BlockSpec(shape, lambda b, i: (0,) * len(shape))
    return pl.pallas_call(
        functools.partial(_ssd_inproj_kernel, tm=tm, chunk=1024),
        grid=(bsz, s // tm),
        in_specs=[rows(d), pl.BlockSpec((1, 6, d), lambda b, i: (b, 0, 0)),
                  pl.BlockSpec(w.shape, lambda b, i: (0, 0), pipeline_mode=pl.Buffered(1)),
                  const((SSM_CONV, SSM_CONV_DIM)), const((1, SSM_CONV_DIM))],
        out_specs=[rows(SSM_D_INNER), rows(SSM_D_INNER), rows(2 * SSM_BC_WIDTH), rows(LANES)],
        out_shape=[jax.ShapeDtypeStruct((bsz, s, SSM_D_INNER), F32),
                   jax.ShapeDtypeStruct((bsz, s, SSM_D_INNER), F32),
                   jax.ShapeDtypeStruct((bsz, s, 2 * SSM_BC_WIDTH), BF16),
                   jax.ShapeDtypeStruct((bsz, s, LANES), F32)],
        scratch_shapes=[pltpu.VMEM((SUBLANES + tm, SSM_CONV_DIM), F32)],
        compiler_params=_compiler_params(("arbitrary", "arbitrary")),
        name="ssd_inproj",
    )(x, mod, w, conv_w.astype(F32), conv_b.reshape(1, -1).astype(F32))


def _ssd_core_kernel(xs_ref, bc_ref, z_ref, dt_ref, dtb_ref, alog_ref, dskip_ref,
                     normw_ref, tri_ref, xp_ref,
                     o_ref, state_ref, *, nchunks):
    ln = SSM_CHUNK

    @pl.when(pl.program_id(1) == 0)
    def _():
        state_ref[...] = jnp.zeros_like(state_ref)

    for chunk in range(nchunks):
        _ssd_chunk(slice(chunk * ln, (chunk + 1) * ln), xs_ref, bc_ref, z_ref, dt_ref, dtb_ref, alog_ref,
                   dskip_ref, normw_ref, tri_ref, xp_ref, o_ref, state_ref)


def _ssd_chunk(rows, xs_ref, bc_ref, z_ref, dt_ref, dtb_ref, alog_ref, dskip_ref, normw_ref, tri_ref,
               xp_ref, o_ref, state_ref):
    ln = SSM_CHUNK
    xs = xs_ref[0, rows, :]

    dt = _softplus(dt_ref[0, rows, :] + dtb_ref[...])
    dt_rows = dt.T
    dta = dt * (-jnp.exp(alog_ref[...]))
    acs = jnp.dot(tri_ref[...], jnp.concatenate(_split3(dta), axis=0), preferred_element_type=F32)
    acs_last = acs[ln - 1:ln, :]
    dtdte = dt * jnp.exp(acs_last - acs)
    acs_rows = acs.T

    def expand(v):
        return jnp.dot(jnp.concatenate(_split3(v), axis=1), xp_ref[...], preferred_element_type=F32)

    xb = xs.astype(BF16)
    xdte = (xs * expand(dtdte)).astype(BF16)
    eacs_x = expand(jnp.exp(acs))

    rr = lax.broadcasted_iota(jnp.int32, (ln, ln), 0)
    cc = lax.broadcasted_iota(jnp.int32, (ln, ln), 1)
    causal = rr >= cc
    lane_grp = lax.broadcasted_iota(jnp.int32, (1, SSM_GROUP_WIDTH), 1)
    zero = jnp.zeros((), BF16)
    nt = (((1,), (1,)), ((), ()))
    tn = (((0,), (0,)), ((), ()))

    for g in range(SSM_GROUPS):
        gs = slice(g * SSM_GROUP_WIDTH, (g + 1) * SSM_GROUP_WIDTH)
        bg = bc_ref[0, rows, g * SSM_STATE:(g + 1) * SSM_STATE]
        cg = bc_ref[0, rows, SSM_BC_WIDTH + g * SSM_STATE:SSM_BC_WIDTH + (g + 1) * SSM_STATE]
        cb = lax.dot_general(cg, bg, nt, preferred_element_type=F32)
        xg = xb[:, gs]
        y = None
        for r in range(SSM_HEADS_PER_GROUP):
            h = g * SSM_HEADS_PER_GROUP + r
            seg = acs[:, h:h + 1] - acs_rows[h:h + 1, :]
            decay = jnp.exp(jnp.where(causal, seg, -jnp.inf))
            mix = (cb * decay * dt_rows[h:h + 1, :]).astype(BF16)
            own = (lane_grp >= r * SSM_HEAD_DIM) & (lane_grp < (r + 1) * SSM_HEAD_DIM)
            part = jnp.dot(mix, jnp.where(own, xg, zero), preferred_element_type=F32)
            y = part if y is None else y + part
        prev = state_ref[g]
        y = y + jnp.dot(cg, prev.astype(BF16), preferred_element_type=F32) * eacs_x[:, gs]
        y = y + xs[:, gs] * dskip_ref[:, gs]
        new_states = lax.dot_general(bg, xdte[:, gs], tn, preferred_element_type=F32)
        state_ref[g] = prev * eacs_x[ln - 1:ln, gs] + new_states
        zg = z_ref[0, rows, gs]
        gated = y * _silu(zg)
        ms = jnp.mean(gated * gated, axis=-1, keepdims=True)
        o_ref[0, rows, gs] = (gated * lax.rsqrt(ms + RMS_EPS) * normw_ref[:, gs]).astype(o_ref.dtype)


def _ssd_constants():
    ln = SSM_CHUNK
    t = jnp.arange(ln)
    tri = (t[:, None] >= t[None, :]).astype(BF16)
    tri3 = jnp.concatenate([tri] * SPLIT, axis=1)
    head = jnp.arange(SPLIT * LANES) % LANES
    col = jnp.arange(SSM_D_INNER)
    xp = ((head < SSM_HEADS)[:, None] & (head[:, None] == (col // SSM_HEAD_DIM)[None, :])).astype(BF16)
    return tri3, xp


def _ssd_core(z, xs, bc, dt, dt_bias, a_log, d_skip, norm_w):
    bsz, s, _ = z.shape
    nchunks = SSD_CHUNKS_PER_STEP
    ln = nchunks * SSM_CHUNK
    pad_heads = lambda v: jnp.pad(v.astype(F32), (0, LANES - SSM_HEADS)).reshape(1, LANES)
    consts = _ssd_constants()
    const = lambda a: pl.BlockSpec(a.shape, lambda b, i: (0,) * a.ndim)
    rows = lambda width: pl.BlockSpec((1, ln, width), lambda b, i: (b, i, 0))
    small = [pad_heads(dt_bias), pad_heads(a_log),
             jnp.repeat(d_skip.astype(F32), SSM_HEAD_DIM).reshape(1, -1), norm_w.reshape(1, -1).astype(F32)]
    return pl.pallas_call(
        functools.partial(_ssd_core_kernel, nchunks=nchunks),
        grid=(bsz, s // ln),
        in_specs=[rows(SSM_D_INNER), rows(2 * SSM_BC_WIDTH), rows(SSM_D_INNER), rows(LANES)]
        + [const(a) for a in small] + [const(a) for a in consts],
        out_specs=rows(SSM_D_INNER),
        out_shape=jax.ShapeDtypeStruct((bsz, s, SSM_D_INNER), BF16),
        scratch_shapes=[pltpu.VMEM((SSM_GROUPS, SSM_STATE, SSM_GROUP_WIDTH), F32)],
        compiler_params=_compiler_params(("arbitrary", "arbitrary")),
        name="ssd_core",
    )(xs, bc, z, dt, *small, *consts)


def kernel(x, c, ada_w, ada_b, ln_mix_g, ln_mix_b, ln_mlp_g, ln_mlp_b, mlp_w1, mlp_w2, fox_w_in, fox_b_f, fox_w_o, ssm_w_in, ssm_conv_w, ssm_conv_b, ssm_dt_bias, ssm_a_log, ssm_d, ssm_norm_w, ssm_w_out):
    mod = _ada_modulation(c, ada_w, ada_b)

    q, k, v, qaug, kaug, rblk, rmid, qnblk, kntile = _fox_inproj(x, mod[0], fox_w_in[0], fox_b_f[0])
    attn = _fox_attention(q, k, v, qaug, kaug, rblk, rmid, qnblk, kntile)
    x = _proj_mlp(attn, fox_w_o[0], x, mod[0], ln_mix_g[0], ln_mix_b[0], mlp_w1[0], mlp_w2[0],
                  ln_mlp_g[0], ln_mlp_b[0])

    z, xs, bc, dt = _ssd_inproj(x, mod[1], ssm_w_in[0], ssm_conv_w[0], ssm_conv_b[0])
    y = _ssd_core(z, xs, bc, dt, ssm_dt_bias[0], ssm_a_log[0], ssm_d[0], ssm_norm_w[0])
    x = _proj_mlp(y, ssm_w_out[0], x, mod[1], ln_mix_g[1], ln_mix_b[1], mlp_w1[1], mlp_w2[1],
                  ln_mlp_g[1], ln_mlp_b[1])
    return x
```

```python
import functools

import jax
import jax.numpy as jnp
from jax import lax
from jax.experimental import pallas as pl
from jax.experimental.pallas import tpu as pltpu

F32 = jnp.float32
BF16 = jnp.bfloat16

D_MODEL = 1024
DEPTH = 2
FOX_HEADS = 16
FOX_HEAD_DIM = D_MODEL // FOX_HEADS
SSM_D_INNER = 2 * D_MODEL
SSM_HEAD_DIM = 64
SSM_HEADS = SSM_D_INNER // SSM_HEAD_DIM
SSM_GROUPS = 8
SSM_HEADS_PER_GROUP = SSM_HEADS // SSM_GROUPS
SSM_STATE = 128
SSM_CONV = 4
SSM_CHUNK = 128
SSM_GROUP_WIDTH = SSM_D_INNER // SSM_GROUPS
SSM_BC_WIDTH = SSM_GROUPS * SSM_STATE
SSM_CONV_DIM = SSM_D_INNER + 2 * SSM_BC_WIDTH
D_FF = 4 * D_MODEL
LN_EPS = 1e-5
RMS_EPS = 1e-5
RESIDUAL_ALPHA = (2.0 * DEPTH) ** 0.25

LANES = 128
SUBLANES = 8
VMEM_LIMIT_BYTES = 56 * 1024 * 1024

ATTN_BLOCK = 512
ATTN_PAIRS_PER_STEP = 2
PREFIX_ROWS = 128
ROW_TILE = 512
SSD_CHUNKS_PER_STEP = 4
AUG_PER_HEAD = 6
SPLIT = 3
NORM_SLACK = 1.01
SKIP_LOG_MARGIN = 32.0
FAST_MAX_GAP = 60.0


def _split3(v):
    hi = v.astype(BF16)
    r1 = v - hi.astype(F32)
    mid = r1.astype(BF16)
    lo = (r1 - mid.astype(F32)).astype(BF16)
    return hi, mid, lo


def _softplus(y):
    return jnp.maximum(y, 0.0) + jnp.log1p(jnp.exp(-jnp.abs(y)))


def _silu(y):
    h = 0.5 * y
    return h + h * jnp.tanh(h)


def _layer_norm(r, g, b):
    mu = jnp.mean(r, axis=-1, keepdims=True)
    cen = r - mu
    var = jnp.mean(cen * cen, axis=-1, keepdims=True)
    return cen * lax.rsqrt(var + LN_EPS) * g + b


def _compiler_params(semantics):
    return pltpu.CompilerParams(dimension_semantics=semantics, vmem_limit_bytes=VMEM_LIMIT_BYTES)


def _ada_kernel(c_ref, w_ref, b_ref, o_ref):
    c = c_ref[...]
    cond = _silu(c)
    o_ref[0] = jnp.dot(cond, w_ref[0], precision=lax.Precision.HIGHEST,
                       preferred_element_type=F32) + b_ref[0]


def _ada_modulation(c, ada_w, ada_b):
    depth, d, n = ada_w.shape
    bsz = c.shape[0]
    tn = 1536
    out = pl.pallas_call(
        _ada_kernel,
        grid=(depth, n // tn),
        in_specs=[
            pl.BlockSpec((bsz, d), lambda i, j: (0, 0)),
            pl.BlockSpec((1, d, tn), lambda i, j: (i, 0, j)),
            pl.BlockSpec((1, 1, tn), lambda i, j: (i, 0, j)),
        ],
        out_specs=pl.BlockSpec((1, bsz, tn), lambda i, j: (i, 0, j)),
        out_shape=jax.ShapeDtypeStruct((depth, bsz, n), F32),
        compiler_params=_compiler_params(("arbitrary", "arbitrary")),
        name="ada_modulation",
    )(c, ada_w, ada_b.reshape(depth, 1, n))
    return out.reshape(depth, bsz, 6, d)


def _fox_inproj_kernel(x_ref, mod_ref, w_ref, bf_ref, tri_ref, eq_ref, ek_ref, oq_ref, ok_ref, hs_ref,
                       q_ref, k_ref, v_ref, qaug_ref, kaug_ref, rblk_ref, rq_ref, qn_ref, kn_ref,
                       carry_ref, kmax_ref, *, tm, gb):
    @pl.when(pl.program_id(1) == 0)
    def _():
        carry_ref[...] = jnp.zeros_like(carry_ref)
        kmax_ref[...] = jnp.zeros_like(kmax_ref)

    x = x_ref[0]
    u = (x * (1.0 + mod_ref[0, 1:2, :]) + mod_ref[0, 0:1, :]).astype(BF16)
    d = D_MODEL
    qb = (jnp.dot(u, w_ref[:, 0:d], preferred_element_type=F32) * (FOX_HEAD_DIM ** -0.5)).astype(BF16)
    kb = jnp.dot(u, w_ref[:, d:2 * d], preferred_element_type=F32).astype(BF16)
    q_ref[0] = qb
    k_ref[0] = kb

    def head_norms(t):
        t = t.astype(F32)
        return jnp.sqrt(jnp.dot((t * t).astype(BF16), hs_ref[...], preferred_element_type=F32)) * NORM_SLACK

    qnorm = head_norms(qb)
    kmax_ref[...] = jnp.maximum(kmax_ref[...], jnp.max(head_norms(kb), axis=0, keepdims=True))
    kn_ref[0, 0] = kmax_ref[...]
    v_ref[0] = jnp.dot(u, w_ref[:, 2 * d:3 * d], preferred_element_type=F32).astype(BF16)
    f = jnp.dot(u, w_ref[:, 3 * d:3 * d + FOX_HEADS], preferred_element_type=F32)
    f = jnp.concatenate([f, jnp.zeros((tm, LANES - FOX_HEADS), F32)], axis=1) + bf_ref[...]
    logf = jnp.minimum(f, 0.0) - jnp.log1p(jnp.exp(-jnp.abs(f)))

    for g in range(tm // gb):
        pieces, run = [], jnp.zeros((1, LANES), F32)
        for k in range(gb // PREFIX_ROWS):
            lf = logf[g * gb + k * PREFIX_ROWS:g * gb + (k + 1) * PREFIX_ROWS]
            part = jnp.dot(tri_ref[...], jnp.concatenate(_split3(lf), axis=0), preferred_element_type=F32) + run
            pieces.append(part)
            run = part[PREFIX_ROWS - 1:PREFIX_ROWS, :]
        rel = jnp.concatenate(pieces, axis=0)
        rcat = jnp.concatenate(_split3(rel), axis=1)
        qaug_ref[0, g * gb:(g + 1) * gb, :] = (
            jnp.dot(rcat, eq_ref[...], preferred_element_type=F32) + oq_ref[...]).astype(BF16)
        kaug_ref[0, g * gb:(g + 1) * gb, :] = (
            jnp.dot(rcat, ek_ref[...], preferred_element_type=F32) + ok_ref[...]).astype(BF16)
        rblk_ref[0, 0, g:g + 1, :] = carry_ref[...]
        for k in range(3):
            row = (k + 1) * gb // 4 - 1
            rq_ref[0, 0, 3 * g + k:3 * g + k + 1, :] = carry_ref[...] + rel[row:row + 1, :]
        qn_ref[0, 0, g:g + 1, :] = jnp.max(qnorm[g * gb:(g + 1) * gb], axis=0, keepdims=True)
        carry_ref[...] = carry_ref[...] + rel[gb - 1:gb, :]


def _fox_aug_constants():
    lane = jnp.arange(LANES)
    row = jnp.arange(SPLIT * LANES)
    piece, head = row // LANES, row % LANES
    valid = head < FOX_HEADS
    eq = (valid[:, None] & (lane[None, :] == (AUG_PER_HEAD * head + piece)[:, None])).astype(BF16)
    ek = -(valid[:, None] & (lane[None, :] == (AUG_PER_HEAD * head + SPLIT + piece)[:, None])).astype(BF16)
    used = lane < AUG_PER_HEAD * FOX_HEADS
    oq = (used & (lane % AUG_PER_HEAD >= SPLIT)).astype(F32)[None, :]
    ok = (used & (lane % AUG_PER_HEAD < SPLIT)).astype(F32)[None, :]
    t = jnp.arange(PREFIX_ROWS)
    tri = (t[:, None] >= t[None, :]).astype(BF16)
    tri3 = jnp.concatenate([tri] * SPLIT, axis=1)
    hs = (jnp.arange(D_MODEL)[:, None] // FOX_HEAD_DIM == lane[None, :]).astype(BF16)
    return tri3, eq, ek, oq, ok, hs


def _fox_inproj(x, mod, w_in, b_f):
    bsz, s, d = x.shape
    tm, gb = min(ROW_TILE, s), ATTN_BLOCK
    w = w_in.astype(BF16)
    bf = jnp.pad(b_f, (0, LANES - FOX_HEADS)).reshape(1, LANES).astype(F32)
    tri3, eq, ek, oq, ok, hs = _fox_aug_constants()
    const = lambda shape: pl.BlockSpec(shape, lambda b, i: (0,) * len(shape))
    rows = lambda width: pl.BlockSpec((1, tm, width), lambda b, i: (b, i, 0))
    per_block = pl.BlockSpec((1, 1, tm // gb, LANES), lambda b, i: (b, i, 0, 0))
    return pl.pallas_call(
        functools.partial(_fox_inproj_kernel, tm=tm, gb=gb),
        grid=(bsz, s // tm),
        in_specs=[rows(d), pl.BlockSpec((1, 6, d), lambda b, i: (b, 0, 0)), const(w.shape),
                  const(bf.shape), const(tri3.shape), const(eq.shape), const(ek.shape),
                  const(oq.shape), const(ok.shape), const(hs.shape)],
        out_specs=[rows(d), rows(d), rows(d), rows(LANES), rows(LANES), per_block,
                   pl.BlockSpec((1, 1, 3 * (tm // gb), LANES), lambda b, i: (b, i, 0, 0)), per_block,
                   pl.BlockSpec((1, 1, 1, LANES), lambda b, i: (b, i, 0, 0))],
        out_shape=[jax.ShapeDtypeStruct((bsz, s, d), BF16)] * 3
        + [jax.ShapeDtypeStruct((bsz, s, LANES), BF16)] * 2
        + [jax.ShapeDtypeStruct((bsz, s // tm, tm // gb, LANES), F32),
           jax.ShapeDtypeStruct((bsz, s // tm, 3 * (tm // gb), LANES), F32),
           jax.ShapeDtypeStruct((bsz, s // tm, tm // gb, LANES), F32)]
        + [jax.ShapeDtypeStruct((bsz, s // tm, 1, LANES), F32)],
        scratch_shapes=[pltpu.VMEM((1, LANES), F32), pltpu.VMEM((1, LANES), F32)],
        compiler_params=_compiler_params(("arbitrary", "arbitrary")),
        name="fox_inproj",
    )(x, mod, w, bf, tri3, eq, ek, oq, ok, hs)


def _fox_attn_kernel(r_ref, rq1_ref, rmid_ref, rq3_ref, qn_ref, kn_ref, q_ref, qaug_ref, k_ref, v_ref, kaug_ref,
                     o_ref, *, blk, nblk, npairs):
    b, hg, i = pl.program_id(0), pl.program_id(1), pl.program_id(2)
    nheads = 2 * npairs
    lane = lax.broadcasted_iota(jnp.int32, (1, LANES), 1)
    qa = qaug_ref[0]
    zero = jnp.zeros((), BF16)
    nt = (((1,), (1,)), ((), ()))
    slab = lambda pair: slice(pair * LANES, (pair + 1) * LANES)

    qcat, own, rbase, ubound, first, far_dead, more_dead = [], [], [], [], [], [], []
    for hh in range(nheads):
        h = nheads * hg + hh
        half = hh % 2
        own.append((lane >= half * FOX_HEAD_DIM) & (lane < (half + 1) * FOX_HEAD_DIM))
        aug = (lane >= AUG_PER_HEAD * h) & (lane < AUG_PER_HEAD * (h + 1))
        qcat.append(jnp.concatenate([jnp.where(own[hh], q_ref[0, :, slab(hh // 2)], zero),
                                     jnp.where(aug, qa, zero)], axis=1))
        base = (b * FOX_HEADS + h) * nblk
        rbase.append(base)
        ub = qn_ref[base + i] * kn_ref[b * FOX_HEADS + h]
        ubound.append(ub)
        thresh = -(SKIP_LOG_MARGIN + 2.0 * ub)
        r_i = r_ref[base + i]
        first.append(lax.while_loop(
            lambda j, base=base, r_i=r_i, thresh=thresh: (j > 0) & (r_i - r_ref[base + j] >= thresh),
            lambda j: j - 1, i))
        rmid_prev = rmid_ref[base + jnp.maximum(i - 1, 0)]
        far_dead.append(rmid_ref[base + i] - rmid_prev < thresh)
        more_dead.append((rq1_ref[base + i] - rmid_prev < thresh) & (rq3_ref[base + i] - r_i < thresh))
    j_first = functools.reduce(jnp.minimum, first)
    fast = 2.0 * functools.reduce(jnp.maximum, ubound) <= FAST_MAX_GAP
    skip_far_quarter = functools.reduce(jnp.logical_and, far_dead)
    skip_more = skip_far_quarter & functools.reduce(jnp.logical_and, more_dead)

    def block_shift(hh, j):
        return r_ref[rbase[hh] + i] - r_ref[rbase[hh] + j]

    def causal_mask(z):
        rr = lax.broadcasted_iota(jnp.int32, z.shape, 0)
        cc = lax.broadcasted_iota(jnp.int32, z.shape, 1)
        return jnp.where(rr >= cc, z, -jnp.inf)

    def key_rows(j):
        return pl.ds(pl.multiple_of(j * blk, blk), blk)

    def key_operand(j, pair):
        return jnp.concatenate([k_ref[0, key_rows(j), slab(pair)], kaug_ref[0, key_rows(j), :]], axis=1)

    @pl.when(fast)
    def _():
        ones_col = [jnp.broadcast_to(jnp.where(lane == half, 1.0, 0.0), (blk, LANES)).astype(BF16)
                    for half in range(2)]

        def tile(j, kind):
            hb = blk // 2
            outs = []
            for pair in range(npairs):
                kcat = key_operand(j, pair)
                vs = v_ref[0, key_rows(j), slab(pair)]
                out = None
                for half in range(2):
                    hh = 2 * pair + half
                    shift = block_shift(hh, j) - ubound[hh]
                    vaug = jnp.concatenate([jnp.where(own[hh], vs, zero), ones_col[half]], axis=1)
                    if kind == "diagonal":
                        zl = lax.dot_general(qcat[hh], kcat[:hb], nt, preferred_element_type=F32) + shift
                        zr = lax.dot_general(qcat[hh][hb:], kcat[hb:], nt, preferred_element_type=F32) + shift
                        part = jnp.dot(jnp.exp(causal_mask(zl)).astype(BF16), vaug[:hb],
                                       preferred_element_type=F32)
                        lower = jnp.dot(jnp.exp(causal_mask(zr)).astype(BF16), vaug[hb:],
                                        preferred_element_type=F32)
                        part = jnp.concatenate([part[:hb], part[hb:] + lower], axis=0)
                    elif kind == "near":
                        zt = lax.dot_general(qcat[hh][:hb], kcat[:hb], nt, preferred_element_type=F32) + shift
                        zr = lax.dot_general(qcat[hh], kcat[hb:], nt, preferred_element_type=F32) + shift
                        part = jnp.dot(jnp.exp(zr).astype(BF16), vaug[hb:], preferred_element_type=F32)
                        upper = jnp.dot(jnp.exp(zt).astype(BF16), vaug[:hb], preferred_element_type=F32)
                        part = jnp.concatenate([part[:hb] + upper, part[hb:]], axis=0)
                    elif kind == "nearest":
                        qb = blk // 4
                        zt = lax.dot_general(qcat[hh][:qb], kcat[:hb], nt, preferred_element_type=F32) + shift
                        zr = lax.dot_general(qcat[hh][:3 * qb], kcat[hb:], nt, preferred_element_type=F32) + shift
                        part = jnp.dot(jnp.exp(zr).astype(BF16), vaug[hb:], preferred_element_type=F32)
                        upper = jnp.dot(jnp.exp(zt).astype(BF16), vaug[:hb], preferred_element_type=F32)
                        part = jnp.concatenate([part[:qb] + upper, part[qb:], jnp.zeros((qb, 2 * LANES), F32)],
                                               axis=0)
                    else:
                        z = lax.dot_general(qcat[hh], kcat, nt, preferred_element_type=F32) + shift
                        part = jnp.dot(jnp.exp(z).astype(BF16), vaug, preferred_element_type=F32)
                    out = part if out is None else out + part
                outs.append(out)
            return tuple(outs)

        def add(accs, parts):
            return tuple(a + p for a, p in zip(accs, parts))

        def finish(accs):
            for pair, acc in enumerate(accs):
                l = jnp.where(own[0], acc[:, LANES:LANES + 1], acc[:, LANES + 1:LANES + 2])
                o_ref[0, :, slab(pair)] = (acc[:, :LANES] / l).astype(o_ref.dtype)

        def with_off_diagonal(nearest_kind):
            init = tuple(jnp.zeros((blk, 2 * LANES), F32) for _ in range(npairs))
            accs = lax.fori_loop(j_first, i - 1, lambda j, a: add(a, tile(j, "full")), init)
            finish(add(add(accs, tile(i - 1, nearest_kind)), tile(i, "diagonal")))

        @pl.when((j_first < i) & skip_more)
        def _():
            with_off_diagonal("nearest")

        @pl.when((j_first < i) & skip_far_quarter & jnp.logical_not(skip_more))
        def _():
            with_off_diagonal("near")

        @pl.when((j_first < i) & jnp.logical_not(skip_far_quarter))
        def _():
            with_off_diagonal("full")

        @pl.when(j_first >= i)
        def _():
            finish(tile(i, "diagonal"))

    @pl.when(jnp.logical_not(fast))
    def _():
        def tile(j, carry, diagonal):
            out = []
            for hh in range(nheads):
                m, l, acc = carry[hh]
                z = lax.dot_general(qcat[hh], key_operand(j, hh // 2), nt, preferred_element_type=F32)
                dij = block_shift(hh, j)
                if diagonal:
                    z = causal_mask(z)
                m_new = jnp.maximum(m, jnp.max(z, axis=1, keepdims=True) + dij)
                p = jnp.exp(z - (m_new - dij))
                alpha = jnp.exp(m - m_new)
                l = alpha * l + jnp.sum(p, axis=1, keepdims=True)
                vm = jnp.where(own[hh], v_ref[0, key_rows(j), slab(hh // 2)], zero)
                acc = alpha * acc + jnp.dot(p.astype(BF16), vm, preferred_element_type=F32)
                out.append((m_new, l, acc))
            return tuple(out)

        init = tuple((jnp.full((blk, 1), -jnp.inf, F32), jnp.zeros((blk, 1), F32),
                      jnp.zeros((blk, LANES), F32)) for _ in range(nheads))
        carry = lax.fori_loop(j_first, i, lambda j, c: tile(j, c, False), init)
        carry = tile(i, carry, True)
        for pair in range(npairs):
            (_, l0, a0), (_, l1, a1) = carry[2 * pair], carry[2 * pair + 1]
            o_ref[0, :, slab(pair)] = (a0 / l0 + a1 / l1).astype(o_ref.dtype)


def _fox_attention(q, k, v, qaug, kaug, rblk, rquarter, qnblk, kntile):
    bsz, s, d = q.shape
    blk = ATTN_BLOCK
    nblk = s // blk
    npairs = ATTN_PAIRS_PER_STEP
    width = npairs * LANES
    flat = lambda a: a.reshape(bsz, nblk, LANES)[:, :, :FOX_HEADS].transpose(0, 2, 1).reshape(-1)
    knflat = kntile[:, -1, 0, :FOX_HEADS].reshape(-1)
    rq = rquarter.reshape(bsz, nblk, 3, LANES)
    idx = lambda f: (lambda b, hg, i, r, r1, r2, r3, qn, kn: f(b, hg, i))
    grid_spec = pltpu.PrefetchScalarGridSpec(
        num_scalar_prefetch=6,
        grid=(bsz, FOX_HEADS // (2 * npairs), nblk),
        in_specs=[
            pl.BlockSpec((1, blk, width), idx(lambda b, hg, i: (b, i, hg))),
            pl.BlockSpec((1, blk, LANES), idx(lambda b, hg, i: (b, i, 0))),
            pl.BlockSpec((1, s, width), idx(lambda b, hg, i: (b, 0, hg))),
            pl.BlockSpec((1, s, width), idx(lambda b, hg, i: (b, 0, hg))),
            pl.BlockSpec((1, s, LANES), idx(lambda b, hg, i: (b, 0, 0))),
        ],
        out_specs=pl.BlockSpec((1, blk, width), idx(lambda b, hg, i: (b, i, hg))),
    )
    return pl.pallas_call(
        functools.partial(_fox_attn_kernel, blk=blk, nblk=nblk, npairs=npairs),
        grid_spec=grid_spec,
        out_shape=jax.ShapeDtypeStruct((bsz, s, d), BF16),
        compiler_params=_compiler_params(("arbitrary", "arbitrary", "arbitrary")),
        name="fox_attention",
    )(flat(rblk), flat(rq[:, :, 0]), flat(rq[:, :, 1]), flat(rq[:, :, 2]), flat(qnblk), knflat,
      q, qaug, k, v, kaug)


def _proj_mlp_kernel(a_ref, wo_ref, x_ref, mod_ref, g1_ref, b1_ref, w1_ref, w2_ref, g2_ref, b2_ref, o_ref,
                     *, ff_chunk):
    y = jnp.dot(a_ref[0], wo_ref[...], preferred_element_type=F32)
    x1 = _layer_norm(RESIDUAL_ALPHA * x_ref[0] + (1.0 + mod_ref[0, 2:3, :]) * y, g1_ref[...], b1_ref[...])
    u = (x1 * (1.0 + mod_ref[0, 4:5, :]) + mod_ref[0, 3:4, :]).astype(BF16)
    y = None
    for c in range(D_FF // ff_chunk):
        h = jnp.dot(u, w1_ref[:, c * ff_chunk:(c + 1) * ff_chunk], preferred_element_type=F32)
        h = jnp.square(jnp.maximum(h, 0.0)).astype(BF16)
        part = jnp.dot(h, w2_ref[c * ff_chunk:(c + 1) * ff_chunk, :], preferred_element_type=F32)
        y = part if y is None else y + part
    r = RESIDUAL_ALPHA * x1 + (1.0 + mod_ref[0, 5:6, :]) * y
    o_ref[0] = _layer_norm(r, g2_ref[...], b2_ref[...])


def _proj_mlp(a, w_o, x, mod, ln1_g, ln1_b, w1, w2, ln2_g, ln2_b):
    bsz, s, d = x.shape
    kdim = a.shape[-1]
    tm = min(ROW_TILE, s)
    rows = lambda width: pl.BlockSpec((1, tm, width), lambda b, i: (b, i, 0))
    resident = lambda shape: pl.BlockSpec(shape, lambda b, i: (0,) * len(shape), pipeline_mode=pl.Buffered(1))
    vec = pl.BlockSpec((1, d), lambda b, i: (0, 0))
    return pl.pallas_call(
        functools.partial(_proj_mlp_kernel, ff_chunk=1024),
        grid=(bsz, s // tm),
        in_specs=[rows(kdim), resident((kdim, d)), rows(d), pl.BlockSpec((1, 6, d), lambda b, i: (b, 0, 0)),
                  vec, vec, resident((d, D_FF)), resident((D_FF, d)), vec, vec],
        out_specs=rows(d),
        out_shape=jax.ShapeDtypeStruct((bsz, s, d), F32),
        compiler_params=_compiler_params(("arbitrary", "arbitrary")),
        name="proj_mlp",
    )(a, w_o.astype(BF16), x, mod, ln1_g.reshape(1, d), ln1_b.reshape(1, d), w1.astype(BF16), w2.astype(BF16),
      ln2_g.reshape(1, d), ln2_b.reshape(1, d))


def _ssd_inproj_kernel(x_ref, mod_ref, w_ref, convw_ref, convb_ref, z_ref, xs_ref, bc_ref, dt_ref, ext_ref,
                       *, tm, chunk):
    halo = SUBLANES

    @pl.when(pl.program_id(1) == 0)
    def _():
        ext_ref[0:halo, :] = jnp.zeros((halo, SSM_CONV_DIM), F32)

    u = (x_ref[0] * (1.0 + mod_ref[0, 1:2, :]) + mod_ref[0, 0:1, :]).astype(BF16)
    for c in range(SSM_D_INNER // chunk):
        z_ref[0, :, c * chunk:(c + 1) * chunk] = jnp.dot(
            u, w_ref[:, c * chunk:(c + 1) * chunk], preferred_element_type=F32)
    for c in range(SSM_CONV_DIM // chunk):
        cols = slice(c * chunk, (c + 1) * chunk)
        ext_ref[halo:halo + tm, cols] = jnp.dot(
            u, w_ref[:, SSM_D_INNER + c * chunk:SSM_D_INNER + (c + 1) * chunk], preferred_element_type=F32)
        ext = ext_ref[:, cols]
        ext1 = pltpu.roll(ext, 1, axis=0)
        near = ext * convw_ref[3:4, cols] + ext1 * convw_ref[2:3, cols]
        far = ext * convw_ref[1:2, cols] + ext1 * convw_ref[0:1, cols]
        conv = (near + pltpu.roll(far, 2, axis=0))[halo:halo + tm] + convb_ref[:, cols]
        ext_ref[0:halo, cols] = ext_ref[tm:tm + halo, cols]
        act = _silu(conv)
        if (c + 1) * chunk <= SSM_D_INNER:
            xs_ref[0, :, cols] = act
        else:
            bc_ref[0, :, c * chunk - SSM_D_INNER:(c + 1) * chunk - SSM_D_INNER] = act.astype(BF16)
    off = SSM_D_INNER + SSM_CONV_DIM
    dt = jnp.dot(u, w_ref[:, off:off + SSM_HEADS], preferred_element_type=F32)
    dt_ref[0] = jnp.concatenate([dt, jnp.zeros((tm, LANES - SSM_HEADS), F32)], axis=1)


def _ssd_inproj(x, mod, w_in, conv_w, conv_b):
    bsz, s, d = x.shape
    tm = min(ROW_TILE, s)
    w = w_in.astype(BF16)
    rows = lambda width: pl.BlockSpec((1, tm, width), lambda b, i: (b, i, 0))
    const = lambda shape: pl.BlockSpec(shape, lambda b, i: (0,) * len(shape))
    return pl.pallas_call(
        functools.partial(_ssd_inproj_kernel, tm=tm, chunk=1024),
        grid=(bsz, s // tm),
        in_specs=[rows(d), pl.BlockSpec((1, 6, d), lambda b, i: (b, 0, 0)),
                  pl.BlockSpec(w.shape, lambda b, i: (0, 0), pipeline_mode=pl.Buffered(1)),
                  const((SSM_CONV, SSM_CONV_DIM)), const((1, SSM_CONV_DIM))],
        out_specs=[rows(SSM_D_INNER), rows(SSM_D_INNER), rows(2 * SSM_BC_WIDTH), rows(LANES)],
        out_shape=[jax.ShapeDtypeStruct((bsz, s, SSM_D_INNER), F32),
                   jax.ShapeDtypeStruct((bsz, s, SSM_D_INNER), F32),
                   jax.ShapeDtypeStruct((bsz, s, 2 * SSM_BC_WIDTH), BF16),
                   jax.ShapeDtypeStruct((bsz, s, LANES), F32)],
        scratch_shapes=[pltpu.VMEM((SUBLANES + tm, SSM_CONV_DIM), F32)],
        compiler_params=_compiler_params(("arbitrary", "arbitrary")),
        name="ssd_inproj",
    )(x, mod, w, conv_w.astype(F32), conv_b.reshape(1, -1).astype(F32))


def _ssd_core_kernel(xs_ref, bc_ref, z_ref, dt_ref, dtb_ref, alog_ref, dskip_ref,
                     normw_ref, tri_ref, xp_ref,
                     o_ref, state_ref, *, nchunks):
    ln = SSM_CHUNK

    @pl.when(pl.program_id(1) == 0)
    def _():
        state_ref[...] = jnp.zeros_like(state_ref)

    for chunk in range(nchunks):
        _ssd_chunk(slice(chunk * ln, (chunk + 1) * ln), xs_ref, bc_ref, z_ref, dt_ref, dtb_ref, alog_ref,
                   dskip_ref, normw_ref, tri_ref, xp_ref, o_ref, state_ref)


def _ssd_chunk(rows, xs_ref, bc_ref, z_ref, dt_ref, dtb_ref, alog_ref, dskip_ref, normw_ref, tri_ref,
               xp_ref, o_ref, state_ref):
    ln = SSM_CHUNK
    xs = xs_ref[0, rows, :]

    dt = _softplus(dt_ref[0, rows, :] + dtb_ref[...])
    dt_rows = dt.T
    dta = dt * (-jnp.exp(alog_ref[...]))
    acs = jnp.dot(tri_ref[...], jnp.concatenate(_split3(dta), axis=0), preferred_element_type=F32)
    acs_last = acs[ln - 1:ln, :]
    dtdte = dt * jnp.exp(acs_last - acs)
    acs_rows = acs.T

    def expand(v):
        return jnp.dot(jnp.concatenate(_split3(v), axis=1), xp_ref[...], preferred_element_type=F32)

    xb = xs.astype(BF16)
    xdte = (xs * expand(dtdte)).astype(BF16)
    eacs_x = expand(jnp.exp(acs))

    rr = lax.broadcasted_iota(jnp.int32, (ln, ln), 0)
    cc = lax.broadcasted_iota(jnp.int32, (ln, ln), 1)
    causal = rr >= cc
    lane_grp = lax.broadcasted_iota(jnp.int32, (1, SSM_GROUP_WIDTH), 1)
    zero = jnp.zeros((), BF16)
    nt = (((1,), (1,)), ((), ()))
    tn = (((0,), (0,)), ((), ()))

    for g in range(SSM_GROUPS):
        gs = slice(g * SSM_GROUP_WIDTH, (g + 1) * SSM_GROUP_WIDTH)
        bg = bc_ref[0, rows, g * SSM_STATE:(g + 1) * SSM_STATE]
        cg = bc_ref[0, rows, SSM_BC_WIDTH + g * SSM_STATE:SSM_BC_WIDTH + (g + 1) * SSM_STATE]
        cb = lax.dot_general(cg, bg, nt, preferred_element_type=F32)
        xg = xb[:, gs]
        y = None
        for r in range(SSM_HEADS_PER_GROUP):
            h = g * SSM_HEADS_PER_GROUP + r
            seg = acs[:, h:h + 1] - acs_rows[h:h + 1, :]
            decay = jnp.exp(jnp.where(causal, seg, -jnp.inf))
            mix = (cb * decay * dt_rows[h:h + 1, :]).astype(BF16)
            own = (lane_grp >= r * SSM_HEAD_DIM) & (lane_grp < (r + 1) * SSM_HEAD_DIM)
            part = jnp.dot(mix, jnp.where(own, xg, zero), preferred_element_type=F32)
            y = part if y is None else y + part
        prev = state_ref[g]
        y = y + jnp.dot(cg, prev.astype(BF16), preferred_element_type=F32) * eacs_x[:, gs]
        y = y + xs[:, gs] * dskip_ref[:, gs]
        new_states = lax.dot_general(bg, xdte[:, gs], tn, preferred_element_type=F32)
        state_ref[g] = prev * eacs_x[ln - 1:ln, gs] + new_states
        zg = z_ref[0, rows, gs]
        gated = y * _silu(zg)
        ms = jnp.mean(gated * gated, axis=-1, keepdims=True)
        o_ref[0, rows, gs] = (gated * lax.rsqrt(ms + RMS_EPS) * normw_ref[:, gs]).astype(o_ref.dtype)


def _ssd_constants():
    ln = SSM_CHUNK
    t = jnp.arange(ln)
    tri = (t[:, None] >= t[None, :]).astype(BF16)
    tri3 = jnp.concatenate([tri] * SPLIT, axis=1)
    head = jnp.arange(SPLIT * LANES) % LANES
    col = jnp.arange(SSM_D_INNER)
    xp = ((head < SSM_HEADS)[:, None] & (head[:, None] == (col // SSM_HEAD_DIM)[None, :])).astype(BF16)
    return tri3, xp


def _ssd_core(z, xs, bc, dt, dt_bias, a_log, d_skip, norm_w):
    bsz, s, _ = z.shape
    nchunks = SSD_CHUNKS_PER_STEP
    ln = nchunks * SSM_CHUNK
    pad_heads = lambda v: jnp.pad(v.astype(F32), (0, LANES - SSM_HEADS)).reshape(1, LANES)
    consts = _ssd_constants()
    const = lambda a: pl.BlockSpec(a.shape, lambda b, i: (0,) * a.ndim)
    rows = lambda width: pl.BlockSpec((1, ln, width), lambda b, i: (b, i, 0))
    small = [pad_heads(dt_bias), pad_heads(a_log),
             jnp.repeat(d_skip.astype(F32), SSM_HEAD_DIM).reshape(1, -1), norm_w.reshape(1, -1).astype(F32)]
    return pl.pallas_call(
        functools.partial(_ssd_core_kernel, nchunks=nchunks),
        grid=(bsz, s // ln),
        in_specs=[rows(SSM_D_INNER), rows(2 * SSM_BC_WIDTH), rows(SSM_D_INNER), rows(LANES)]
        + [const(a) for a in small] + [const(a) for a in consts],
        out_specs=rows(SSM_D_INNER),
        out_shape=jax.ShapeDtypeStruct((bsz, s, SSM_D_INNER), BF16),
        scratch_shapes=[pltpu.VMEM((SSM_GROUPS, SSM_STATE, SSM_GROUP_WIDTH), F32)],
        compiler_params=_compiler_params(("arbitrary", "arbitrary")),
        name="ssd_core",
    )(xs, bc, z, dt, *small, *consts)


def kernel(x, c, ada_w, ada_b, ln_mix_g, ln_mix_b, ln_mlp_g, ln_mlp_b, mlp_w1, mlp_w2, fox_w_in, fox_b_f, fox_w_o, ssm_w_in, ssm_conv_w, ssm_conv_b, ssm_dt_bias, ssm_a_log, ssm_d, ssm_norm_w, ssm_w_out):
    mod = _ada_modulation(c, ada_w, ada_b)

    q, k, v, qaug, kaug, rblk, rquarter, qnblk, kntile = _fox_inproj(x, mod[0], fox_w_in[0], fox_b_f[0])
    attn = _fox_attention(q, k, v, qaug, kaug, rblk, rquarter, qnblk, kntile)
    x = _proj_mlp(attn, fox_w_o[0], x, mod[0], ln_mix_g[0], ln_mix_b[0], mlp_w1[0], mlp_w2[0],
                  ln_mlp_g[0], ln_mlp_b[0])

    z, xs, bc, dt = _ssd_inproj(x, mod[1], ssm_w_in[0], ssm_conv_w[0], ssm_conv_b[0])
    y = _ssd_core(z, xs, bc, dt, ssm_dt_bias[0], ssm_a_log[0], ssm_d[0], ssm_norm_w[0])
    x = _proj_mlp(y, ssm_w_out[0], x, mod[1], ln_mix_g[1], ln_mix_b[1], mlp_w1[1], mlp_w2[1],
                  ln_mlp_g[1], ln_mlp_b[1])
    return x
```

```python
import functools

import jax
import jax.numpy as jnp
from jax import lax
from jax.experimental import pallas as pl
from jax.experimental.pallas import tpu as pltpu

F32 = jnp.float32
BF16 = jnp.bfloat16

D_MODEL = 1024
DEPTH = 2
FOX_HEADS = 16
FOX_HEAD_DIM = D_MODEL // FOX_HEADS
SSM_D_INNER = 2 * D_MODEL
SSM_HEAD_DIM = 64
SSM_HEADS = SSM_D_INNER // SSM_HEAD_DIM
SSM_GROUPS = 8
SSM_HEADS_PER_GROUP = SSM_HEADS // SSM_GROUPS
SSM_STATE = 128
SSM_CONV = 4
SSM_CHUNK = 128
SSM_GROUP_WIDTH = SSM_D_INNER // SSM_GROUPS
SSM_BC_WIDTH = SSM_GROUPS * SSM_STATE
SSM_CONV_DIM = SSM_D_INNER + 2 * SSM_BC_WIDTH
D_FF = 4 * D_MODEL
LN_EPS = 1e-5
RMS_EPS = 1e-5
RESIDUAL_ALPHA = (2.0 * DEPTH) ** 0.25

LANES = 128
SUBLANES = 8
VMEM_LIMIT_BYTES = 56 * 1024 * 1024

ATTN_BLOCK = 512
ATTN_PAIRS_PER_STEP = 2
PREFIX_ROWS = 128
ROW_TILE = 512
SSD_CHUNKS_PER_STEP = 4
AUG_PER_HEAD = 6
SPLIT = 3
NORM_SLACK = 1.01
SKIP_LOG_MARGIN = 32.0
FAST_MAX_GAP = 60.0


def _split3(v):
    hi = v.astype(BF16)
    r1 = v - hi.astype(F32)
    mid = r1.astype(BF16)
    lo = (r1 - mid.astype(F32)).astype(BF16)
    return hi, mid, lo


def _softplus(y):
    return jnp.maximum(y, 0.0) + jnp.log1p(jnp.exp(-jnp.abs(y)))


def _silu(y):
    h = 0.5 * y
    return h + h * jnp.tanh(h)


def _layer_norm(r, g, b):
    mu = jnp.mean(r, axis=-1, keepdims=True)
    cen = r - mu
    var = jnp.mean(cen * cen, axis=-1, keepdims=True)
    return cen * lax.rsqrt(var + LN_EPS) * g + b


def _compiler_params(semantics):
    return pltpu.CompilerParams(dimension_semantics=semantics, vmem_limit_bytes=VMEM_LIMIT_BYTES)


def _ada_kernel(c_ref, w_ref, b_ref, o_ref):
    c = c_ref[...]
    cond = _silu(c)
    o_ref[0] = jnp.dot(cond, w_ref[0], precision=lax.Precision.HIGHEST,
                       preferred_element_type=F32) + b_ref[0]


def _ada_modulation(c, ada_w, ada_b):
    depth, d, n = ada_w.shape
    bsz = c.shape[0]
    tn = 1536
    out = pl.pallas_call(
        _ada_kernel,
        grid=(depth, n // tn),
        in_specs=[
            pl.BlockSpec((bsz, d), lambda i, j: (0, 0)),
            pl.BlockSpec((1, d, tn), lambda i, j: (i, 0, j)),
            pl.BlockSpec((1, 1, tn), lambda i, j: (i, 0, j)),
        ],
        out_specs=pl.BlockSpec((1, bsz, tn), lambda i, j: (i, 0, j)),
        out_shape=jax.ShapeDtypeStruct((depth, bsz, n), F32),
        compiler_params=_compiler_params(("arbitrary", "arbitrary")),
        name="ada_modulation",
    )(c, ada_w, ada_b.reshape(depth, 1, n))
    return out.reshape(depth, bsz, 6, d)


def _fox_inproj_kernel(x_ref, mod_ref, w_ref, bf_ref, tri_ref, eq_ref, ek_ref, oq_ref, ok_ref, hs_ref,
                       q_ref, k_ref, v_ref, qaug_ref, kaug_ref, rblk_ref, rq_ref, qn_ref, kn_ref,
                       carry_ref, kmax_ref, *, tm, gb):
    @pl.when(pl.program_id(1) == 0)
    def _():
        carry_ref[...] = jnp.zeros_like(carry_ref)
        kmax_ref[...] = jnp.zeros_like(kmax_ref)

    x = x_ref[0]
    u = (x * (1.0 + mod_ref[0, 1:2, :]) + mod_ref[0, 0:1, :]).astype(BF16)
    d = D_MODEL
    qb = (jnp.dot(u, w_ref[:, 0:d], preferred_element_type=F32) * (FOX_HEAD_DIM ** -0.5)).astype(BF16)
    kb = jnp.dot(u, w_ref[:, d:2 * d], preferred_element_type=F32).astype(BF16)
    q_ref[0] = qb
    k_ref[0] = kb

    def head_norms(t):
        t = t.astype(F32)
        return jnp.sqrt(jnp.dot((t * t).astype(BF16), hs_ref[...], preferred_element_type=F32)) * NORM_SLACK

    qnorm = head_norms(qb)
    kmax_ref[...] = jnp.maximum(kmax_ref[...], jnp.max(head_norms(kb), axis=0, keepdims=True))
    kn_ref[0, 0] = kmax_ref[...]
    v_ref[0] = jnp.dot(u, w_ref[:, 2 * d:3 * d], preferred_element_type=F32).astype(BF16)
    f = jnp.dot(u, w_ref[:, 3 * d:3 * d + FOX_HEADS], preferred_element_type=F32)
    f = jnp.concatenate([f, jnp.zeros((tm, LANES - FOX_HEADS), F32)], axis=1) + bf_ref[...]
    logf = jnp.minimum(f, 0.0) - jnp.log1p(jnp.exp(-jnp.abs(f)))

    for g in range(tm // gb):
        pieces, run = [], jnp.zeros((1, LANES), F32)
        for k in range(gb // PREFIX_ROWS):
            lf = logf[g * gb + k * PREFIX_ROWS:g * gb + (k + 1) * PREFIX_ROWS]
            part = jnp.dot(tri_ref[...], jnp.concatenate(_split3(lf), axis=0), preferred_element_type=F32) + run
            pieces.append(part)
            run = part[PREFIX_ROWS - 1:PREFIX_ROWS, :]
        rel = jnp.concatenate(pieces, axis=0)
        rcat = jnp.concatenate(_split3(rel), axis=1)
        qaug_ref[0, g * gb:(g + 1) * gb, :] = (
            jnp.dot(rcat, eq_ref[...], preferred_element_type=F32) + oq_ref[...]).astype(BF16)
        kaug_ref[0, g * gb:(g + 1) * gb, :] = (
            jnp.dot(rcat, ek_ref[...], preferred_element_type=F32) + ok_ref[...]).astype(BF16)
        rblk_ref[0, 0, g:g + 1, :] = carry_ref[...]
        for k in range(3):
            row = (k + 1) * gb // 4 - 1
            rq_ref[0, 0, 3 * g + k:3 * g + k + 1, :] = carry_ref[...] + rel[row:row + 1, :]
        qn_ref[0, 0, g:g + 1, :] = jnp.max(qnorm[g * gb:(g + 1) * gb], axis=0, keepdims=True)
        carry_ref[...] = carry_ref[...] + rel[gb - 1:gb, :]


def _fox_aug_constants():
    lane = jnp.arange(LANES)
    row = jnp.arange(SPLIT * LANES)
    piece, head = row // LANES, row % LANES
    valid = head < FOX_HEADS
    eq = (valid[:, None] & (lane[None, :] == (AUG_PER_HEAD * head + piece)[:, None])).astype(BF16)
    ek = -(valid[:, None] & (lane[None, :] == (AUG_PER_HEAD * head + SPLIT + piece)[:, None])).astype(BF16)
    used = lane < AUG_PER_HEAD * FOX_HEADS
    oq = (used & (lane % AUG_PER_HEAD >= SPLIT)).astype(F32)[None, :]
    ok = (used & (lane % AUG_PER_HEAD < SPLIT)).astype(F32)[None, :]
    t = jnp.arange(PREFIX_ROWS)
    tri = (t[:, None] >= t[None, :]).astype(BF16)
    tri3 = jnp.concatenate([tri] * SPLIT, axis=1)
    hs = (jnp.arange(D_MODEL)[:, None] // FOX_HEAD_DIM == lane[None, :]).astype(BF16)
    return tri3, eq, ek, oq, ok, hs


def _fox_inproj(x, mod, w_in, b_f):
    bsz, s, d = x.shape
    tm, gb = min(ROW_TILE, s), ATTN_BLOCK
    w = w_in.astype(BF16)
    bf = jnp.pad(b_f, (0, LANES - FOX_HEADS)).reshape(1, LANES).astype(F32)
    tri3, eq, ek, oq, ok, hs = _fox_aug_constants()
    const = lambda shape: pl.BlockSpec(shape, lambda b, i: (0,) * len(shape))
    rows = lambda width: pl.BlockSpec((1, tm, width), lambda b, i: (b, i, 0))
    per_block = pl.BlockSpec((1, 1, tm // gb, LANES), lambda b, i: (b, i, 0, 0))
    return pl.pallas_call(
        functools.partial(_fox_inproj_kernel, tm=tm, gb=gb),
        grid=(bsz, s // tm),
        in_specs=[rows(d), pl.BlockSpec((1, 6, d), lambda b, i: (b, 0, 0)), const(w.shape),
                  const(bf.shape), const(tri3.shape), const(eq.shape), const(ek.shape),
                  const(oq.shape), const(ok.shape), const(hs.shape)],
        out_specs=[rows(d), rows(d), rows(d), rows(LANES), rows(LANES), per_block,
                   pl.BlockSpec((1, 1, 3 * (tm // gb), LANES), lambda b, i: (b, i, 0, 0)), per_block,
                   pl.BlockSpec((1, 1, 1, LANES), lambda b, i: (b, i, 0, 0))],
        out_shape=[jax.ShapeDtypeStruct((bsz, s, d), BF16)] * 3
        + [jax.ShapeDtypeStruct((bsz, s, LANES), BF16)] * 2
        + [jax.ShapeDtypeStruct((bsz, s // tm, tm // gb, LANES), F32),
           jax.ShapeDtypeStruct((bsz, s // tm, 3 * (tm // gb), LANES), F32),
           jax.ShapeDtypeStruct((bsz, s // tm, tm // gb, LANES), F32)]
        + [jax.ShapeDtypeStruct((bsz, s // tm, 1, LANES), F32)],
        scratch_shapes=[pltpu.VMEM((1, LANES), F32), pltpu.VMEM((1, LANES), F32)],
        compiler_params=_compiler_params(("arbitrary", "arbitrary")),
        name="fox_inproj",
    )(x, mod, w, bf, tri3, eq, ek, oq, ok, hs)


def _fox_attn_kernel(r_ref, rq1_ref, rmid_ref, rq3_ref, qn_ref, kn_ref, q_ref, qaug_ref, k_ref, v_ref, kaug_ref,
                     o_ref, *, blk, nblk, npairs):
    b, hg, i = pl.program_id(0), pl.program_id(1), pl.program_id(2)
    nheads = 2 * npairs
    lane = lax.broadcasted_iota(jnp.int32, (1, LANES), 1)
    qa = qaug_ref[0]
    zero = jnp.zeros((), BF16)
    nt = (((1,), (1,)), ((), ()))
    slab = lambda pair: slice(pair * LANES, (pair + 1) * LANES)

    qcat, own, rbase, ubound, first, far_dead, more_dead = [], [], [], [], [], [], []
    for hh in range(nheads):
        h = nheads * hg + hh
        half = hh % 2
        own.append((lane >= half * FOX_HEAD_DIM) & (lane < (half + 1) * FOX_HEAD_DIM))
        aug = (lane >= AUG_PER_HEAD * h) & (lane < AUG_PER_HEAD * (h + 1))
        qcat.append(jnp.concatenate([jnp.where(own[hh], q_ref[0, :, slab(hh // 2)], zero),
                                     jnp.where(aug, qa, zero)], axis=1))
        base = (b * FOX_HEADS + h) * nblk
        rbase.append(base)
        ub = qn_ref[base + i] * kn_ref[b * FOX_HEADS + h]
        ubound.append(ub)
        thresh = -(SKIP_LOG_MARGIN + 2.0 * ub)
        r_i = r_ref[base + i]
        first.append(lax.while_loop(
            lambda j, base=base, r_i=r_i, thresh=thresh: (j > 0) & (r_i - r_ref[base + j] >= thresh),
            lambda j: j - 1, i))
        rmid_prev = rmid_ref[base + jnp.maximum(i - 1, 0)]
        far_dead.append(rmid_ref[base + i] - rmid_prev < thresh)
        more_dead.append((rq1_ref[base + i] - rmid_prev < thresh) & (rq3_ref[base + i] - r_i < thresh))
    j_first = functools.reduce(jnp.minimum, first)
    fast = 2.0 * functools.reduce(jnp.maximum, ubound) <= FAST_MAX_GAP
    skip_far_quarter = functools.reduce(jnp.logical_and, far_dead)
    skip_more = skip_far_quarter & functools.reduce(jnp.logical_and, more_dead)

    def block_shift(hh, j):
        return r_ref[rbase[hh] + i] - r_ref[rbase[hh] + j]

    def causal_mask(z):
        rr = lax.broadcasted_iota(jnp.int32, z.shape, 0)
        cc = lax.broadcasted_iota(jnp.int32, z.shape, 1)
        return jnp.where(rr >= cc, z, -jnp.inf)

    def key_rows(j):
        return pl.ds(pl.multiple_of(j * blk, blk), blk)

    def key_operand(j, pair):
        return jnp.concatenate([k_ref[0, key_rows(j), slab(pair)], kaug_ref[0, key_rows(j), :]], axis=1)

    @pl.when(fast)
    def _():
        ones_col = [jnp.broadcast_to(jnp.where(lane == half, 1.0, 0.0), (blk, LANES)).astype(BF16)
                    for half in range(2)]

        def tile(j, kind):
            hb = blk // 2
            outs = []
            for pair in range(npairs):
                kcat = key_operand(j, pair)
                vs = v_ref[0, key_rows(j), slab(pair)]
                out = None
                for half in range(2):
                    hh = 2 * pair + half
                    shift = block_shift(hh, j) - ubound[hh]
                    vaug = jnp.concatenate([jnp.where(own[hh], vs, zero), ones_col[half]], axis=1)
                    if kind == "diagonal":
                        zl = lax.dot_general(qcat[hh], kcat[:hb], nt, preferred_element_type=F32) + shift
                        zr = lax.dot_general(qcat[hh][hb:], kcat[hb:], nt, preferred_element_type=F32) + shift
                        part = jnp.dot(jnp.exp(causal_mask(zl)).astype(BF16), vaug[:hb],
                                       preferred_element_type=F32)
                        lower = jnp.dot(jnp.exp(causal_mask(zr)).astype(BF16), vaug[hb:],
                                        preferred_element_type=F32)
                        part = jnp.concatenate([part[:hb], part[hb:] + lower], axis=0)
                    elif kind == "near":
                        zt = lax.dot_general(qcat[hh][:hb], kcat[:hb], nt, preferred_element_type=F32) + shift
                        zr = lax.dot_general(qcat[hh], kcat[hb:], nt, preferred_element_type=F32) + shift
                        part = jnp.dot(jnp.exp(zr).astype(BF16), vaug[hb:], preferred_element_type=F32)
                        upper = jnp.dot(jnp.exp(zt).astype(BF16), vaug[:hb], preferred_element_type=F32)
                        part = jnp.concatenate([part[:hb] + upper, part[hb:]], axis=0)
                    elif kind == "nearest":
                        qb = blk // 4
                        zt = lax.dot_general(qcat[hh][:qb], kcat[:hb], nt, preferred_element_type=F32) + shift
                        zr = lax.dot_general(qcat[hh][:3 * qb], kcat[hb:], nt, preferred_element_type=F32) + shift
                        part = jnp.dot(jnp.exp(zr).astype(BF16), vaug[hb:], preferred_element_type=F32)
                        upper = jnp.dot(jnp.exp(zt).astype(BF16), vaug[:hb], preferred_element_type=F32)
                        part = jnp.concatenate([part[:qb] + upper, part[qb:], jnp.zeros((qb, 2 * LANES), F32)],
                                               axis=0)
                    else:
                        z = lax.dot_general(qcat[hh], kcat, nt, preferred_element_type=F32) + shift
                        part = jnp.dot(jnp.exp(z).astype(BF16), vaug, preferred_element_type=F32)
                    out = part if out is None else out + part
                outs.append(out)
            return tuple(outs)

        def add(accs, parts):
            return tuple(a + p for a, p in zip(accs, parts))

        def finish(accs):
            for pair, acc in enumerate(accs):
                l = jnp.where(own[0], acc[:, LANES:LANES + 1], acc[:, LANES + 1:LANES + 2])
                o_ref[0, :, slab(pair)] = (acc[:, :LANES] / l).astype(o_ref.dtype)

        def with_off_diagonal(nearest_kind):
            init = tuple(jnp.zeros((blk, 2 * LANES), F32) for _ in range(npairs))
            accs = lax.fori_loop(j_first, i - 1, lambda j, a: add(a, tile(j, "full")), init)
            finish(add(add(accs, tile(i - 1, nearest_kind)), tile(i, "diagonal")))

        @pl.when((j_first < i) & skip_more)
        def _():
            with_off_diagonal("nearest")

        @pl.when((j_first < i) & skip_far_quarter & jnp.logical_not(skip_more))
        def _():
            with_off_diagonal("near")

        @pl.when((j_first < i) & jnp.logical_not(skip_far_quarter))
        def _():
            with_off_diagonal("full")

        @pl.when(j_first >= i)
        def _():
            finish(tile(i, "diagonal"))

    @pl.when(jnp.logical_not(fast))
    def _():
        def tile(j, carry, diagonal):
            out = []
            for hh in range(nheads):
                m, l, acc = carry[hh]
                z = lax.dot_general(qcat[hh], key_operand(j, hh // 2), nt, preferred_element_type=F32)
                dij = block_shift(hh, j)
                if diagonal:
                    z = causal_mask(z)
                m_new = jnp.maximum(m, jnp.max(z, axis=1, keepdims=True) + dij)
                p = jnp.exp(z - (m_new - dij))
                alpha = jnp.exp(m - m_new)
                l = alpha * l + jnp.sum(p, axis=1, keepdims=True)
                vm = jnp.where(own[hh], v_ref[0, key_rows(j), slab(hh // 2)], zero)
                acc = alpha * acc + jnp.dot(p.astype(BF16), vm, preferred_element_type=F32)
                out.append((m_new, l, acc))
            return tuple(out)

        init = tuple((jnp.full((blk, 1), -jnp.inf, F32), jnp.zeros((blk, 1), F32),
                      jnp.zeros((blk, LANES), F32)) for _ in range(nheads))
        carry = lax.fori_loop(j_first, i, lambda j, c: tile(j, c, False), init)
        carry = tile(i, carry, True)
        for pair in range(npairs):
            (_, l0, a0), (_, l1, a1) = carry[2 * pair], carry[2 * pair + 1]
            o_ref[0, :, slab(pair)] = (a0 / l0 + a1 / l1).astype(o_ref.dtype)


def _fox_attention(q, k, v, qaug, kaug, rblk, rquarter, qnblk, kntile):
    bsz, s, d = q.shape
    blk = ATTN_BLOCK
    nblk = s // blk
    npairs = ATTN_PAIRS_PER_STEP
    width = npairs * LANES
    flat = lambda a: a.reshape(bsz, nblk, LANES)[:, :, :FOX_HEADS].transpose(0, 2, 1).reshape(-1)
    knflat = kntile[:, -1, 0, :FOX_HEADS].reshape(-1)
    rq = rquarter.reshape(bsz, nblk, 3, LANES)
    idx = lambda f: (lambda b, hg, i, r, r1, r2, r3, qn, kn: f(b, hg, i))
    grid_spec = pltpu.PrefetchScalarGridSpec(
        num_scalar_prefetch=6,
        grid=(bsz, FOX_HEADS // (2 * npairs), nblk),
        in_specs=[
            pl.BlockSpec((1, blk, width), idx(lambda b, hg, i: (b, i, hg))),
            pl.BlockSpec((1, blk, LANES), idx(lambda b, hg, i: (b, i, 0))),
            pl.BlockSpec((1, s, width), idx(lambda b, hg, i: (b, 0, hg))),
            pl.BlockSpec((1, s, width), idx(lambda b, hg, i: (b, 0, hg))),
            pl.BlockSpec((1, s, LANES), idx(lambda b, hg, i: (b, 0, 0))),
        ],
        out_specs=pl.BlockSpec((1, blk, width), idx(lambda b, hg, i: (b, i, hg))),
    )
    return pl.pallas_call(
        functools.partial(_fox_attn_kernel, blk=blk, nblk=nblk, npairs=npairs),
        grid_spec=grid_spec,
        out_shape=jax.ShapeDtypeStruct((bsz, s, d), BF16),
        compiler_params=_compiler_params(("arbitrary", "arbitrary", "arbitrary")),
        name="fox_attention",
    )(flat(rblk), flat(rq[:, :, 0]), flat(rq[:, :, 1]), flat(rq[:, :, 2]), flat(qnblk), knflat,
      q, qaug, k, v, kaug)


def _proj_mlp_kernel(a_ref, wo_ref, x_ref, mod_ref, g1_ref, b1_ref, w1_ref, w2_ref, g2_ref, b2_ref, o_ref,
                     *, ff_chunk):
    y = jnp.dot(a_ref[0], wo_ref[...], preferred_element_type=F32)
    x1 = _layer_norm(RESIDUAL_ALPHA * x_ref[0] + (1.0 + mod_ref[0, 2:3, :]) * y, g1_ref[...], b1_ref[...])
    u = (x1 * (1.0 + mod_ref[0, 4:5, :]) + mod_ref[0, 3:4, :]).astype(BF16)
    y = None
    for c in range(D_FF // ff_chunk):
        h = jnp.dot(u, w1_ref[:, c * ff_chunk:(c + 1) * ff_chunk], preferred_element_type=F32)
        h = jnp.square(jnp.maximum(h, 0.0)).astype(BF16)
        part = jnp.dot(h, w2_ref[c * ff_chunk:(c + 1) * ff_chunk, :], preferred_element_type=F32)
        y = part if y is None else y + part
    r = RESIDUAL_ALPHA * x1 + (1.0 + mod_ref[0, 5:6, :]) * y
    o_ref[0] = _layer_norm(r, g2_ref[...], b2_ref[...])


def _proj_mlp(a, w_o, x, mod, ln1_g, ln1_b, w1, w2, ln2_g, ln2_b):
    bsz, s, d = x.shape
    kdim = a.shape[-1]
    tm = min(ROW_TILE, s)
    rows = lambda width: pl.BlockSpec((1, tm, width), lambda b, i: (b, i, 0))
    resident = lambda shape: pl.BlockSpec(shape, lambda b, i: (0,) * len(shape), pipeline_mode=pl.Buffered(1))
    vec = pl.BlockSpec((1, d), lambda b, i: (0, 0))
    return pl.pallas_call(
        functools.partial(_proj_mlp_kernel, ff_chunk=1024),
        grid=(bsz, s // tm),
        in_specs=[rows(kdim), resident((kdim, d)), rows(d), pl.BlockSpec((1, 6, d), lambda b, i: (b, 0, 0)),
                  vec, vec, resident((d, D_FF)), resident((D_FF, d)), vec, vec],
        out_specs=rows(d),
        out_shape=jax.ShapeDtypeStruct((bsz, s, d), F32),
        compiler_params=_compiler_params(("arbitrary", "arbitrary")),
        name="proj_mlp",
    )(a, w_o.astype(BF16), x, mod, ln1_g.reshape(1, d), ln1_b.reshape(1, d), w1.astype(BF16), w2.astype(BF16),
      ln2_g.reshape(1, d), ln2_b.reshape(1, d))


def _ssd_inproj_kernel(x_ref, mod_ref, w_ref, convw_ref, convb_ref, z_ref, xs_ref, bc_ref, dt_ref, ext_ref,
                       *, tm, chunk):
    halo = SUBLANES

    @pl.when(pl.program_id(1) == 0)
    def _():
        ext_ref[0:halo, :] = jnp.zeros((halo, SSM_CONV_DIM), F32)

    u = (x_ref[0] * (1.0 + mod_ref[0, 1:2, :]) + mod_ref[0, 0:1, :]).astype(BF16)
    for c in range(SSM_D_INNER // chunk):
        z_ref[0, :, c * chunk:(c + 1) * chunk] = jnp.dot(
            u, w_ref[:, c * chunk:(c + 1) * chunk], preferred_element_type=F32)
    for c in range(SSM_CONV_DIM // chunk):
        cols = slice(c * chunk, (c + 1) * chunk)
        ext_ref[halo:halo + tm, cols] = jnp.dot(
            u, w_ref[:, SSM_D_INNER + c * chunk:SSM_D_INNER + (c + 1) * chunk], preferred_element_type=F32)
        ext = ext_ref[:, cols]
        ext1 = pltpu.roll(ext, 1, axis=0)
        near = ext * convw_ref[3:4, cols] + ext1 * convw_ref[2:3, cols]
        far = ext * convw_ref[1:2, cols] + ext1 * convw_ref[0:1, cols]
        conv = (near + pltpu.roll(far, 2, axis=0))[halo:halo + tm] + convb_ref[:, cols]
        ext_ref[0:halo, cols] = ext_ref[tm:tm + halo, cols]
        act = _silu(conv)
        if (c + 1) * chunk <= SSM_D_INNER:
            xs_ref[0, :, cols] = act
        else:
            bc_ref[0, :, c * chunk - SSM_D_INNER:(c + 1) * chunk - SSM_D_INNER] = act.astype(BF16)
    off = SSM_D_INNER + SSM_CONV_DIM
    dt = jnp.dot(u, w_ref[:, off:off + SSM_HEADS], preferred_element_type=F32)
    dt_ref[0] = jnp.concatenate([dt, jnp.zeros((tm, LANES - SSM_HEADS), F32)], axis=1)


def _ssd_inproj(x, mod, w_in, conv_w, conv_b):
    bsz, s, d = x.shape
    tm = min(ROW_TILE, s)
    w = w_in.astype(BF16)
    rows = lambda width: pl.BlockSpec((1, tm, width), lambda b, i: (b, i, 0))
    const = lambda shape: pl.BlockSpec(shape, lambda b, i: (0,) * len(shape))
    return pl.pallas_call(
        functools.partial(_ssd_inproj_kernel, tm=tm, chunk=1024),
        grid=(bsz, s // tm),
        in_specs=[rows(d), pl.BlockSpec((1, 6, d), lambda b, i: (b, 0, 0)),
                  pl.BlockSpec(w.shape, lambda b, i: (0, 0), pipeline_mode=pl.Buffered(1)),
                  const((SSM_CONV, SSM_CONV_DIM)), const((1, SSM_CONV_DIM))],
        out_specs=[rows(SSM_D_INNER), rows(SSM_D_INNER), rows(2 * SSM_BC_WIDTH), rows(LANES)],
        out_shape=[jax.ShapeDtypeStruct((bsz, s, SSM_D_INNER), F32),
                   jax.ShapeDtypeStruct((bsz, s, SSM_D_INNER), F32),
                   jax.ShapeDtypeStruct((bsz, s, 2 * SSM_BC_WIDTH), BF16),
                   jax.ShapeDtypeStruct((bsz, s, LANES), F32)],
        scratch_shapes=[pltpu.VMEM((SUBLANES + tm, SSM_CONV_DIM), F32)],
        compiler_params=_compiler_params(("arbitrary", "arbitrary")),
        name="ssd_inproj",
    )(x, mod, w, conv_w.astype(F32), conv_b.reshape(1, -1).astype(F32))


def _ssd_core_kernel(xs_ref, bc_ref, z_ref, dt_ref, dtb_ref, alog_ref, dskip_ref,
                     normw_ref, tri_ref, xp_ref,
                     o_ref, state_ref, *, nchunks):
    ln = SSM_CHUNK

    @pl.when(pl.program_id(1) == 0)
    def _():
        state_ref[...] = jnp.zeros_like(state_ref)

    for chunk in range(nchunks):
        _ssd_chunk(slice(chunk * ln, (chunk + 1) * ln), xs_ref, bc_ref, z_ref, dt_ref, dtb_ref, alog_ref,
                   dskip_ref, normw_ref, tri_ref, xp_ref, o_ref, state_ref)


def _ssd_chunk(rows, xs_ref, bc_ref, z_ref, dt_ref, dtb_ref, alog_ref, dskip_ref, normw_ref, tri_ref,
               xp_ref, o_ref, state_ref):
    ln = SSM_CHUNK
    xs = xs_ref[0, rows, :]

    dt = _softplus(dt_ref[0, rows, :] + dtb_ref[...])
    dt_rows = dt.T
    dta = dt * (-jnp.exp(alog_ref[...]))
    acs = jnp.dot(tri_ref[...], jnp.concatenate(_split3(dta), axis=0), preferred_element_type=F32)
    acs_last = acs[ln - 1:ln, :]
    dtdte = dt * jnp.exp(acs_last - acs)
    acs_rows = acs.T

    lane_h = lax.broadcasted_iota(jnp.int32, (1, LANES), 1)

    def expand(v):
        hi = v.astype(BF16).astype(F32)
        r1 = v - hi
        mid = r1.astype(BF16).astype(F32)
        lo = r1 - mid
        packed = jnp.where(lane_h < SSM_HEADS, hi,
                           jnp.where(lane_h < 2 * SSM_HEADS, pltpu.roll(mid, SSM_HEADS, axis=1),
                                     pltpu.roll(lo, 2 * SSM_HEADS, axis=1)))
        return jnp.dot(packed.astype(BF16), xp_ref[...], preferred_element_type=F32)

    xb = xs.astype(BF16)
    xdte = (xs * expand(dtdte)).astype(BF16)
    eacs_x = expand(jnp.exp(acs))

    rr = lax.broadcasted_iota(jnp.int32, (ln, ln), 0)
    cc = lax.broadcasted_iota(jnp.int32, (ln, ln), 1)
    causal = rr >= cc
    lane_grp = lax.broadcasted_iota(jnp.int32, (1, SSM_GROUP_WIDTH), 1)
    zero = jnp.zeros((), BF16)
    nt = (((1,), (1,)), ((), ()))
    tn = (((0,), (0,)), ((), ()))

    for g in range(SSM_GROUPS):
        gs = slice(g * SSM_GROUP_WIDTH, (g + 1) * SSM_GROUP_WIDTH)
        bg = bc_ref[0, rows, g * SSM_STATE:(g + 1) * SSM_STATE]
        cg = bc_ref[0, rows, SSM_BC_WIDTH + g * SSM_STATE:SSM_BC_WIDTH + (g + 1) * SSM_STATE]
        cb = lax.dot_general(cg, bg, nt, preferred_element_type=F32)
        xg = xb[:, gs]
        y = None
        for r in range(SSM_HEADS_PER_GROUP):
            h = g * SSM_HEADS_PER_GROUP + r
            seg = acs[:, h:h + 1] - acs_rows[h:h + 1, :]
            decay = jnp.exp(jnp.where(causal, seg, -jnp.inf))
            mix = (cb * decay * dt_rows[h:h + 1, :]).astype(BF16)
            own = (lane_grp >= r * SSM_HEAD_DIM) & (lane_grp < (r + 1) * SSM_HEAD_DIM)
            part = jnp.dot(mix, jnp.where(own, xg, zero), preferred_element_type=F32)
            y = part if y is None else y + part
        prev = state_ref[g]
        y = y + jnp.dot(cg, prev.astype(BF16), preferred_element_type=F32) * eacs_x[:, gs]
        y = y + xs[:, gs] * dskip_ref[:, gs]
        new_states = lax.dot_general(bg, xdte[:, gs], tn, preferred_element_type=F32)
        state_ref[g] = prev * eacs_x[ln - 1:ln, gs] + new_states
        zg = z_ref[0, rows, gs]
        gated = y * _silu(zg)
        ms = jnp.mean(gated * gated, axis=-1, keepdims=True)
        o_ref[0, rows, gs] = (gated * lax.rsqrt(ms + RMS_EPS) * normw_ref[:, gs]).astype(o_ref.dtype)


def _ssd_constants():
    ln = SSM_CHUNK
    t = jnp.arange(ln)
    tri = (t[:, None] >= t[None, :]).astype(BF16)
    tri3 = jnp.concatenate([tri] * SPLIT, axis=1)
    row = jnp.arange(LANES)
    col = jnp.arange(SSM_D_INNER)
    xp = ((row < SPLIT * SSM_HEADS)[:, None]
          & ((row % SSM_HEADS)[:, None] == (col // SSM_HEAD_DIM)[None, :])).astype(BF16)
    return tri3, xp


def _ssd_core(z, xs, bc, dt, dt_bias, a_log, d_skip, norm_w):
    bsz, s, _ = z.shape
    nchunks = SSD_CHUNKS_PER_STEP
    ln = nchunks * SSM_CHUNK
    pad_heads = lambda v: jnp.pad(v.astype(F32), (0, LANES - SSM_HEADS)).reshape(1, LANES)
    consts = _ssd_constants()
    const = lambda a: pl.BlockSpec(a.shape, lambda b, i: (0,) * a.ndim)
    rows = lambda width: pl.BlockSpec((1, ln, width), lambda b, i: (b, i, 0))
    small = [pad_heads(dt_bias), pad_heads(a_log),
             jnp.repeat(d_skip.astype(F32), SSM_HEAD_DIM).reshape(1, -1), norm_w.reshape(1, -1).astype(F32)]
    return pl.pallas_call(
        functools.partial(_ssd_core_kernel, nchunks=nchunks),
        grid=(bsz, s // ln),
        in_specs=[rows(SSM_D_INNER), rows(2 * SSM_BC_WIDTH), rows(SSM_D_INNER), rows(LANES)]
        + [const(a) for a in small] + [const(a) for a in consts],
        out_specs=rows(SSM_D_INNER),
        out_shape=jax.ShapeDtypeStruct((bsz, s, SSM_D_INNER), BF16),
        scratch_shapes=[pltpu.VMEM((SSM_GROUPS, SSM_STATE, SSM_GROUP_WIDTH), F32)],
        compiler_params=_compiler_params(("arbitrary", "arbitrary")),
        name="ssd_core",
    )(xs, bc, z, dt, *small, *consts)


def kernel(x, c, ada_w, ada_b, ln_mix_g, ln_mix_b, ln_mlp_g, ln_mlp_b, mlp_w1, mlp_w2, fox_w_in, fox_b_f, fox_w_o, ssm_w_in, ssm_conv_w, ssm_conv_b, ssm_dt_bias, ssm_a_log, ssm_d, ssm_norm_w, ssm_w_out):
    mod = _ada_modulation(c, ada_w, ada_b)

    q, k, v, qaug, kaug, rblk, rquarter, qnblk, kntile = _fox_inproj(x, mod[0], fox_w_in[0], fox_b_f[0])
    attn = _fox_attention(q, k, v, qaug, kaug, rblk, rquarter, qnblk, kntile)
    x = _proj_mlp(attn, fox_w_o[0], x, mod[0], ln_mix_g[0], ln_mix_b[0], mlp_w1[0], mlp_w2[0],
                  ln_mlp_g[0], ln_mlp_b[0])

    z, xs, bc, dt = _ssd_inproj(x, mod[1], ssm_w_in[0], ssm_conv_w[0], ssm_conv_b[0])
    y = _ssd_core(z, xs, bc, dt, ssm_dt_bias[0], ssm_a_log[0], ssm_d[0], ssm_norm_w[0])
    x = _proj_mlp(y, ssm_w_out[0], x, mod[1], ln_mix_g[1], ln_mix_b[1], mlp_w1[1], mlp_w2[1],
                  ln_mlp_g[1], ln_mlp_b[1])
    return x
```
